```python
import math
import jax, jax.numpy as jnp
from jax import lax
import numpy as np

D_MODEL = 1024
BATCH = 32
SEQ = 2048
DEPTH = 4

HEAD_DIM = 64
A_HEADS = D_MODEL // 256
A_V_DIM = 2 * HEAD_DIM
B_HEADS = D_MODEL // 128
ROT_DIM = HEAD_DIM // 4
ROPE_THETA = 500000.0
GRID_W = 64
WIN_R = 8
WIN_C = 16
MLA_HEADS = D_MODEL // 64
MLA_NOPE = 64
MLA_ROPE = 32
MLA_V = 64
Q_LORA = D_MODEL // 4
KV_LORA = D_MODEL // 8
D_FF = 4 * D_MODEL
QBLK = 128
EPS = 1e-5
N_EVEN = (DEPTH + 1) // 2
N_ODD = DEPTH // 2
A_QK = A_HEADS * 2 * HEAD_DIM
A_V = A_HEADS * A_V_DIM
B_W = B_HEADS * HEAD_DIM
EVEN_IN = 2 * A_QK + A_V + 3 * B_W
EVEN_OUT = A_V + B_W
MLA_IN = Q_LORA + KV_LORA + MLA_ROPE

kernel_name = "hybrid_diff_natten_mla_encoder"


def rmsnorm(x, g):
    xf = x.astype(jnp.float32)
    y = xf * lax.rsqrt(jnp.mean(jnp.square(xf), axis=-1, keepdims=True) + EPS)
    return (y * g.astype(jnp.float32)).astype(x.dtype)


def rotary_tables(S, rot_dim):
    pos = jnp.arange(S, dtype=jnp.float32)
    inv = ROPE_THETA ** (-jnp.arange(0, rot_dim, 2, dtype=jnp.float32) / rot_dim)
    ang = pos[:, None] * inv[None, :]
    return jnp.cos(ang)[:, None, :], jnp.sin(ang)[:, None, :]


def apply_rope(x, cos, sin):
    rot = 2 * cos.shape[-1]
    xf = x[..., :rot].astype(jnp.float32)
    x1, x2 = xf[..., : rot // 2], xf[..., rot // 2:]
    xr = jnp.concatenate([x1 * cos - x2 * sin, x2 * cos + x1 * sin], axis=-1).astype(x.dtype)
    return jnp.concatenate([xr, x[..., rot:]], axis=-1)


def diff_attention(q, k, v, lam):
    _, B, H, S, d = q.shape
    nb = S // QBLK
    qb = q.reshape(2, B, H, nb, QBLK, d).transpose(3, 0, 1, 2, 4, 5)
    scale = d ** -0.5

    def block(qi):
        s = jnp.einsum('nbhqd,nbhkd->nbhqk', qi, k).astype(jnp.float32) * scale
        p = jax.nn.softmax(s, axis=-1)
        a = p[0] - lam * p[1]
        return jnp.einsum('bhqk,bhkd->bhqd', a.astype(v.dtype), v)

    o = lax.map(block, qb)
    return o.transpose(1, 2, 0, 3, 4).reshape(B, H, S, -1)


def neighborhood_attention(q, k, v, rpb):
    B, H, S, d = q.shape
    rows = S // GRID_W
    wr = min(WIN_R, rows)
    wc = min(WIN_C, GRID_W)
    qg = q.reshape(B, H, rows, GRID_W, d).transpose(2, 0, 1, 3, 4)
    kg = k.reshape(B, H, rows, GRID_W, d)
    vg = v.reshape(B, H, rows, GRID_W, d)
    c = np.arange(GRID_W)
    cs = np.clip(c - wc // 2, 0, GRID_W - wc)
    col_mask = (c[None, :] >= cs[:, None]) & (c[None, :] < cs[:, None] + wc)
    col_idx = np.clip(c[None, :] - c[:, None], -(WIN_C - 1), WIN_C - 1) + (WIN_C - 1)
    mask = jnp.asarray(col_mask)[:, None, :]
    scale = d ** -0.5

    def row_block(args):
        qr, r = args
        r0 = jnp.clip(r - wr // 2, 0, rows - wr)
        kb = lax.dynamic_slice_in_dim(kg, r0, wr, axis=2)
        vb = lax.dynamic_slice_in_dim(vg, r0, wr, axis=2)
        s = jnp.einsum('bhqd,bhiwd->bhqiw', qr, kb).astype(jnp.float32) * scale
        row_idx = r0 + jnp.arange(wr) - r + (WIN_R - 1)
        bias = rpb[:, row_idx[None, :, None], col_idx[:, None, :]].astype(jnp.float32)
        s = jnp.where(mask, s + bias, -1e30)
        p = jax.nn.softmax(s.reshape(B, H, GRID_W, wr * GRID_W), axis=-1).reshape(s.shape)
        return jnp.einsum('bhqiw,bhiwd->bhqd', p.astype(vb.dtype), vb)

    o = lax.map(row_block, (qg, jnp.arange(rows, dtype=jnp.int32)))
    return o.transpose(1, 2, 0, 3, 4).reshape(B, H, S, d)


def mla_attention(q_nope, q_rope, k_nope, k_rope, v):
    B, H, S, dn = q_nope.shape
    dr = q_rope.shape[-1]
    nb = S // QBLK
    qn = q_nope.reshape(B, H, nb, QBLK, dn).transpose(2, 0, 1, 3, 4)
    qr = q_rope.reshape(B, H, nb, QBLK, dr).transpose(2, 0, 1, 3, 4)
    scale = (dn + dr) ** -0.5

    def block(args):
        qni, qri = args
        s = (jnp.einsum('bhqd,bhkd->bhqk', qni, k_nope)
             + jnp.einsum('bhqr,bkr->bhqk', qri, k_rope)).astype(jnp.float32) * scale
        p = jax.nn.softmax(s, axis=-1)
        return jnp.einsum('bhqk,bhkd->bhqd', p.astype(v.dtype), v)

    o = lax.map(block, (qn, qr))
    return o.transpose(1, 2, 0, 3, 4).reshape(B, H, S, -1)


def even_mixer(h, w_in, w_out, lq1, lk1, lq2, lk2, subln_g, rpb, lam_init):
    B, S, _ = h.shape
    proj = h @ w_in
    splits = np.cumsum([A_QK, A_QK, A_V, B_W, B_W])
    a_q, a_k, a_v, b_q, b_k, b_v = jnp.split(proj, splits, axis=-1)
    cos, sin = rotary_tables(S, ROT_DIM)
    a_q = apply_rope(a_q.reshape(B, S, 2 * A_HEADS, HEAD_DIM), cos, sin)
    a_k = apply_rope(a_k.reshape(B, S, 2 * A_HEADS, HEAD_DIM), cos, sin)
    a_q = a_q.reshape(B, S, A_HEADS, 2, HEAD_DIM).transpose(3, 0, 2, 1, 4)
    a_k = a_k.reshape(B, S, A_HEADS, 2, HEAD_DIM).transpose(3, 0, 2, 1, 4)
    a_v = a_v.reshape(B, S, A_HEADS, A_V_DIM).transpose(0, 2, 1, 3)
    f32 = jnp.float32
    lam = (jnp.exp(jnp.sum(lq1.astype(f32) * lk1.astype(f32)))
           - jnp.exp(jnp.sum(lq2.astype(f32) * lk2.astype(f32))) + lam_init)
    o_a = diff_attention(a_q, a_k, a_v, lam)
    o_a = rmsnorm(o_a, subln_g) * (1.0 - lam_init)
    o_a = o_a.transpose(0, 2, 1, 3).reshape(B, S, A_V)
    to_heads = lambda t: t.reshape(B, S, B_HEADS, HEAD_DIM).transpose(0, 2, 1, 3)
    o_b = neighborhood_attention(to_heads(b_q), to_heads(b_k), to_heads(b_v), rpb)
    o_b = o_b.transpose(0, 2, 1, 3).reshape(B, S, B_W)
    return jnp.concatenate([o_a, o_b], axis=-1) @ w_out


def odd_mixer(h, w_in, q_norm_g, kv_norm_g, w_uq, w_ukv, w_out):
    B, S, _ = h.shape
    proj = h @ w_in
    c_q, c_kv, k_r = jnp.split(proj, [Q_LORA, Q_LORA + KV_LORA], axis=-1)
    cos, sin = rotary_tables(S, MLA_ROPE)
    q = (rmsnorm(c_q, q_norm_g) @ w_uq).reshape(B, S, MLA_HEADS, MLA_NOPE + MLA_ROPE)
    q_nope = q[..., :MLA_NOPE]
    q_rope = apply_rope(q[..., MLA_NOPE:], cos, sin)
    kv = (rmsnorm(c_kv, kv_norm_g) @ w_ukv).reshape(B, S, MLA_HEADS, MLA_NOPE + MLA_V)
    k_nope, v = kv[..., :MLA_NOPE], kv[..., MLA_NOPE:]
    k_rope = apply_rope(k_r[:, :, None, :], cos, sin)[:, :, 0, :]
    th = lambda t: t.transpose(0, 2, 1, 3)
    o = mla_attention(th(q_nope), th(q_rope), th(k_nope), k_rope, th(v))
    return o.transpose(0, 2, 1, 3).reshape(B, S, MLA_HEADS * MLA_V) @ w_out


def setup_inputs(seed: int = 0) -> dict:
    key = jax.random.key(seed)
    ks = jax.random.split(key, 24)
    f32 = jnp.float32
    nrm = lambda k, shape, s: jax.random.normal(k, shape, f32) * s
    gain = lambda k, shape: 1.0 + 0.05 * jax.random.normal(k, shape, f32)
    return {
        "x": nrm(ks[0], (BATCH, SEQ, D_MODEL), 1.0),
        "ln_mix_g": gain(ks[1], (DEPTH, D_MODEL)),
        "ln_mlp_g": gain(ks[2], (DEPTH, D_MODEL)),
        "w_up": nrm(ks[3], (DEPTH, D_MODEL, D_FF), D_MODEL ** -0.5),
        "w_down": nrm(ks[4], (DEPTH, D_FF, D_MODEL), D_FF ** -0.5),
        "ln_f_g": gain(ks[5], (D_MODEL,)),
        "ev_w_in": nrm(ks[6], (N_EVEN, D_MODEL, EVEN_IN), D_MODEL ** -0.5),
        "ev_w_out": nrm(ks[7], (N_EVEN, EVEN_OUT, D_MODEL), EVEN_OUT ** -0.5),
        "ev_lambda_q1": nrm(ks[8], (N_EVEN, HEAD_DIM), 0.1),
        "ev_lambda_k1": nrm(ks[9], (N_EVEN, HEAD_DIM), 0.1),
        "ev_lambda_q2": nrm(ks[10], (N_EVEN, HEAD_DIM), 0.1),
        "ev_lambda_k2": nrm(ks[11], (N_EVEN, HEAD_DIM), 0.1),
        "ev_subln_g": gain(ks[12], (N_EVEN, A_V_DIM)),
        "ev_rpb": nrm(ks[13], (N_EVEN, B_HEADS, 2 * WIN_R - 1, 2 * WIN_C - 1), 0.02),
        "od_w_in": nrm(ks[14], (N_ODD, D_MODEL, MLA_IN), D_MODEL ** -0.5),
        "od_q_norm_g": gain(ks[15], (N_ODD, Q_LORA)),
        "od_kv_norm_g": gain(ks[16], (N_ODD, KV_LORA)),
        "od_w_uq": nrm(ks[17], (N_ODD, Q_LORA, MLA_HEADS * (MLA_NOPE + MLA_ROPE)), Q_LORA ** -0.5),
        "od_w_ukv": nrm(ks[18], (N_ODD, KV_LORA, MLA_HEADS * (MLA_NOPE + MLA_V)), KV_LORA ** -0.5),
        "od_w_out": nrm(ks[19], (N_ODD, MLA_HEADS * MLA_V, D_MODEL), (MLA_HEADS * MLA_V) ** -0.5),
    }


def reference(x, ln_mix_g, ln_mlp_g, w_up, w_down, ln_f_g,
              ev_w_in, ev_w_out, ev_lambda_q1, ev_lambda_k1, ev_lambda_q2, ev_lambda_k2,
              ev_subln_g, ev_rpb,
              od_w_in, od_q_norm_g, od_kv_norm_g, od_w_uq, od_w_ukv, od_w_out):
    for l in range(DEPTH):
        i = l // 2
        h = rmsnorm(x, ln_mix_g[l])
        if l % 2 == 0:
            lam_init = 0.8 - 0.6 * math.exp(-0.3 * l)
            x = x + even_mixer(h, ev_w_in[i], ev_w_out[i], ev_lambda_q1[i], ev_lambda_k1[i],
                               ev_lambda_q2[i], ev_lambda_k2[i], ev_subln_g[i], ev_rpb[i], lam_init)
        else:
            x = x + odd_mixer(h, od_w_in[i], od_q_norm_g[i], od_kv_norm_g[i],
                              od_w_uq[i], od_w_ukv[i], od_w_out[i])
        h = rmsnorm(x, ln_mlp_g[l])
        x = x + jnp.square(jax.nn.relu(h @ w_up[l])) @ w_down[l]
    return rmsnorm(x, ln_f_g)
```

```python
import functools
import math

import jax
import jax.numpy as jnp
import numpy as np
from jax import lax
from jax.experimental import pallas as pl
from jax.experimental.pallas import tpu as pltpu

D_MODEL = 1024
DEPTH = 4
HEAD_DIM = 64
A_HEADS = D_MODEL // 256
A_V_DIM = 2 * HEAD_DIM
B_HEADS = D_MODEL // 128
ROT_DIM = HEAD_DIM // 4
ROPE_THETA = 500000.0
GRID_W = 64
WIN_R = 8
WIN_C = 16
MLA_HEADS = D_MODEL // 64
MLA_NOPE = 64
MLA_ROPE = 32
MLA_V = 64
Q_LORA = D_MODEL // 4
KV_LORA = D_MODEL // 8
D_FF = 4 * D_MODEL
EPS = 1e-5
A_QK = A_HEADS * 2 * HEAD_DIM
A_V = A_HEADS * A_V_DIM
B_W = B_HEADS * HEAD_DIM
EVEN_IN = 2 * A_QK + A_V + 3 * B_W

LANES = 128
EVEN_SLABS = EVEN_IN // LANES
MLA_PAIRS = MLA_HEADS // 2
VMEM_LIMIT_BYTES = 56 * 1024 * 1024

ROW_TILE = 512
FF_CHUNK = 1024
Q_TILE = 256

F32 = jnp.float32
BF16 = jnp.bfloat16


def _params():
    return pltpu.CompilerParams(vmem_limit_bytes=VMEM_LIMIT_BYTES)


def _rms(x, g):
    return x * lax.rsqrt(jnp.mean(x * x, axis=-1, keepdims=True) + EPS) * g


def _rope(x, c, sa, sb, half):
    return x * c + pltpu.roll(x, half, 1) * sa + pltpu.roll(x, LANES - half, 1) * sb


def _rope_tables(seq, rot_dim, lane_of_dim0, period):
    half = rot_dim // 2
    pos = np.arange(seq, dtype=np.float32)
    inv = (np.float32(ROPE_THETA) ** (-np.arange(0, rot_dim, 2, dtype=np.float32) / np.float32(rot_dim))).astype(np.float32)
    pos = jnp.asarray(pos)
    ang = pos[:, None] * jnp.asarray(inv)[None, :]
    cos, sin = jnp.cos(ang), jnp.sin(ang)
    c = jnp.ones((seq, period), F32)
    sa = jnp.zeros((seq, period), F32)
    sb = jnp.zeros((seq, period), F32)
    lo = lane_of_dim0
    c = c.at[:, lo:lo + half].set(cos).at[:, lo + half:lo + rot_dim].set(cos)
    sb = sb.at[:, lo:lo + half].set(-sin)
    sa = sa.at[:, lo + half:lo + rot_dim].set(sin)
    rep = LANES // period
    return tuple(jnp.tile(t, (1, rep)) for t in (c, sa, sb))


def _even_proj_kernel(x_ref, g_ref, w_ref, c_ref, sa_ref, sb_ref, o_ref):
    h = _rms(x_ref[0], g_ref[...]).astype(BF16)
    c, sa, sb = c_ref[...], sa_ref[...], sb_ref[...]
    group = 4
    for j0 in range(0, EVEN_SLABS, group):
        p = jnp.dot(h, w_ref[:, j0 * LANES:(j0 + group) * LANES], preferred_element_type=F32)
        for jj in range(group):
            j = j0 + jj
            blk = p[:, jj * LANES:(jj + 1) * LANES]
            if j < 2 * A_HEADS:
                blk = _rope(blk, c, sa, sb, ROT_DIM // 2)
            if j < A_HEADS or 3 * A_HEADS <= j < 4 * A_HEADS:
                blk = blk * (HEAD_DIM ** -0.5)
            o_ref[0, j] = blk.astype(BF16)


def _even_proj(x, g, w, tables, layer, idx):
    B, S, _ = x.shape
    nt = S // ROW_TILE
    tab_spec = pl.BlockSpec((ROW_TILE, LANES), lambda b, t: (t, 0))
    return pl.pallas_call(
        _even_proj_kernel,
        grid=(B, nt),
        in_specs=[
            pl.BlockSpec((1, ROW_TILE, D_MODEL), lambda b, t: (b, t, 0)),
            pl.BlockSpec((None, 1, D_MODEL), lambda b, t: (layer, 0, 0)),
            pl.BlockSpec((None, D_MODEL, EVEN_IN), lambda b, t: (idx, 0, 0)),
            tab_spec, tab_spec, tab_spec,
        ],
        out_specs=pl.BlockSpec((1, EVEN_SLABS, ROW_TILE, LANES), lambda b, t: (b, 0, t, 0)),
        out_shape=jax.ShapeDtypeStruct((B, EVEN_SLABS, S, LANES), BF16),
        compiler_params=_params(),
        name="even_proj",
    )(x, g, w, *tables)


def _softmax_pv(q, k, v):
    s = lax.dot_general(q, k, (((1,), (1,)), ((), ())), preferred_element_type=F32)
    m = jnp.max(s, axis=-1, keepdims=True)
    p = jnp.exp(s - m)
    l = jnp.sum(p, axis=-1, keepdims=True)
    return jnp.dot(p.astype(BF16), v, preferred_element_type=F32) / l


def _low_lanes(rows):
    return lax.broadcasted_iota(jnp.int32, (rows, LANES), 1) < HEAD_DIM


def _diff_attn_kernel(q_ref, k_ref, v_ref, lp_ref, g_ref, o_ref, *, lam_init, seq):
    lp = lp_ref[...]
    lam = (jnp.exp(jnp.sum(lp[0:1] * lp[1:2], axis=-1, keepdims=True))
           - jnp.exp(jnp.sum(lp[2:3] * lp[3:4], axis=-1, keepdims=True)) + lam_init)
    lo = _low_lanes(Q_TILE)
    gain = g_ref[...] * (1.0 - lam_init)

    def body(i, carry):
        r = pl.multiple_of(i * Q_TILE, Q_TILE)
        q = q_ref[0, 0, pl.ds(r, Q_TILE), :]
        zero = jnp.zeros_like(q)
        o1 = _softmax_pv(jnp.where(lo, q, zero), k_ref[0, 0], v_ref[0, 0])
        o2 = _softmax_pv(jnp.where(lo, zero, q), k_ref[0, 0], v_ref[0, 0])
        o = o1 - lam * o2
        o_ref[0, pl.ds(r, Q_TILE), :] = _rms(o, gain).astype(BF16)
        return carry

    lax.fori_loop(0, seq // Q_TILE, body, 0)


def _diff_attn(proj, lam_params, subln_g, lam_init):
    B, _, S, _ = proj.shape
    slab = lambda off: pl.BlockSpec((1, 1, S, LANES), lambda b, h: (b, off + h, 0, 0))
    return pl.pallas_call(
        functools.partial(_diff_attn_kernel, lam_init=lam_init, seq=S),
        grid=(B, A_HEADS),
        in_specs=[
            slab(0), slab(A_HEADS), slab(2 * A_HEADS),
            pl.BlockSpec((4, HEAD_DIM), lambda b, h: (0, 0)),
            pl.BlockSpec((1, A_V_DIM), lambda b, h: (0, 0)),
        ],
        out_specs=pl.BlockSpec((1, S, LANES), lambda b, h: (b, 0, h)),
        out_shape=jax.ShapeDtypeStruct((B, S, A_V), BF16),
        compiler_params=_params(),
        name="diff_attn",
    )(proj, proj, proj, lam_params, subln_g)


def _natten_bias(rpb):
    c = np.arange(GRID_W)
    cs = np.clip(c - WIN_C // 2, 0, GRID_W - WIN_C)
    col_mask = (c[None, :] >= cs[:, None]) & (c[None, :] < cs[:, None] + WIN_C)
    col_idx = np.clip(c[None, :] - c[:, None], -(WIN_C - 1), WIN_C - 1) + (WIN_C - 1)
    rel = np.arange(WIN_R)
    row_idx = np.arange(WIN_R)[None, :] - rel[:, None] + (WIN_R - 1)
    b = rpb[:, row_idx[:, None, :, None], col_idx[None, :, None, :]]
    b = jnp.where(jnp.asarray(col_mask)[None, None, :, None, :], b.astype(F32), -1e30)
    return b.reshape(rpb.shape[0], WIN_R, GRID_W, WIN_R * GRID_W)


def _natten_kernel(q_ref, k_ref, v_ref, b_ref, o_ref, *, rows):
    lo = _low_lanes(GRID_W)
    win = WIN_R * GRID_W

    def body(r, carry):
        r0 = jnp.clip(r - WIN_R // 2, 0, rows - WIN_R)
        rel = r - r0
        q = q_ref[0, 0, pl.ds(pl.multiple_of(r * GRID_W, GRID_W), GRID_W), :]
        kstart = pl.multiple_of(r0 * GRID_W, GRID_W)
        ks = k_ref[0, 0, pl.ds(kstart, win), :]
        vs = v_ref[0, 0, pl.ds(kstart, win), :]
        zero = jnp.zeros_like(q)
        outs = []
        for hd in range(2):
            qm = jnp.where(lo, q, zero) if hd == 0 else jnp.where(lo, zero, q)
            s = lax.dot_general(qm, ks, (((1,), (1,)), ((), ())), preferred_element_type=F32)
            s = s + b_ref[hd, rel]
            m = jnp.max(s, axis=-1, keepdims=True)
            p = jnp.exp(s - m)
            l = jnp.sum(p, axis=-1, keepdims=True)
            outs.append(jnp.dot(p.astype(BF16), vs, preferred_element_type=F32) / l)
        o_ref[0, pl.ds(pl.multiple_of(r * GRID_W, GRID_W), GRID_W), :] = jnp.where(lo, outs[0], outs[1]).astype(BF16)
        return carry

    lax.fori_loop(0, rows, body, 0)


def _natten(proj, bias):
    B, _, S, _ = proj.shape
    rows = S // GRID_W
    assert rows >= WIN_R
    base = 3 * A_HEADS
    pairs = B_HEADS // 2
    slab = lambda off: pl.BlockSpec((1, 1, S, LANES), lambda p, b: (b, off + p, 0, 0))
    return pl.pallas_call(
        functools.partial(_natten_kernel, rows=rows),
        grid=(pairs, B),
        in_specs=[
            slab(base), slab(base + pairs), slab(base + 2 * pairs),
            pl.BlockSpec((2, WIN_R, GRID_W, WIN_R * GRID_W), lambda p, b: (p, 0, 0, 0)),
        ],
        out_specs=pl.BlockSpec((1, S, LANES), lambda p, b: (b, 0, p)),
        out_shape=jax.ShapeDtypeStruct((B, S, B_W), BF16),
        compiler_params=_params(),
        name="natten",
    )(proj, proj, proj, bias)


def _mla_proj_kernel(x_ref, g_ref, win_ref, gq_ref, gkv_ref, wuq_ref, wuk_ref, wuv_ref,
                     c_ref, sa_ref, sb_ref, q_ref, k_ref, v_ref):
    h = _rms(x_ref[0], g_ref[...]).astype(BF16)
    c, sa, sb = c_ref[...], sa_ref[...], sb_ref[...]
    lat = jnp.dot(h, win_ref[...], preferred_element_type=F32)
    cq = _rms(lat[:, :Q_LORA], gq_ref[...]).astype(BF16)
    ckv = _rms(lat[:, Q_LORA:Q_LORA + KV_LORA], gkv_ref[...]).astype(BF16)
    k_rope = _rope(lat[:, Q_LORA + KV_LORA:], c, sa, sb, MLA_ROPE // 2)
    scale = (MLA_NOPE + MLA_ROPE) ** -0.5
    group = 4
    for j0 in range(0, MLA_HEADS, group):
        cols = slice(j0 * LANES, (j0 + group) * LANES)
        qg = jnp.dot(cq, wuq_ref[:, cols], preferred_element_type=F32)
        kg = jnp.dot(ckv, wuk_ref[:, cols], preferred_element_type=F32)
        for jj in range(group):
            sl = slice(jj * LANES, (jj + 1) * LANES)
            q_ref[0, j0 + jj] = (_rope(qg[:, sl], c, sa, sb, MLA_ROPE // 2) * scale).astype(BF16)
            k_ref[0, j0 + jj] = (kg[:, sl] + k_rope).astype(BF16)
    vv = jnp.dot(ckv, wuv_ref[...], preferred_element_type=F32)
    for j in range(MLA_PAIRS):
        v_ref[0, j] = vv[:, j * LANES:(j + 1) * LANES].astype(BF16)


def _mla_proj(x, g, w_in, gq, gkv, w_uq, w_uk, w_uv, tables, layer, idx):
    B, S, _ = x.shape
    nt = S // ROW_TILE
    tab_spec = pl.BlockSpec((ROW_TILE, LANES), lambda b, t: (t, 0))
    whole = lambda a: pl.BlockSpec((None,) + a.shape[1:], lambda b, t: (idx,) + (0,) * (a.ndim - 1))
    head_out = lambda n: pl.BlockSpec((1, n, ROW_TILE, LANES), lambda b, t: (b, 0, t, 0))
    return pl.pallas_call(
        _mla_proj_kernel,
        grid=(B, nt),
        in_specs=[
            pl.BlockSpec((1, ROW_TILE, D_MODEL), lambda b, t: (b, t, 0)),
            pl.BlockSpec((None, 1, D_MODEL), lambda b, t: (layer, 0, 0)),
            whole(w_in), whole(gq), whole(gkv), whole(w_uq), whole(w_uk), whole(w_uv),
            tab_spec, tab_spec, tab_spec,
        ],
        out_specs=[head_out(MLA_HEADS), head_out(MLA_HEADS), head_out(MLA_PAIRS)],
        out_shape=[
            jax.ShapeDtypeStruct((B, MLA_HEADS, S, LANES), BF16),
            jax.ShapeDtypeStruct((B, MLA_HEADS, S, LANES), BF16),
            jax.ShapeDtypeStruct((B, MLA_PAIRS, S, LANES), BF16),
        ],
        compiler_params=_params(),
        name="mla_proj",
    )(x, g, w_in, gq, gkv, w_uq, w_uk, w_uv, *tables)


def _mla_attn_kernel(q_ref, k_ref, v_ref, o_ref, *, seq):
    lo = _low_lanes(Q_TILE)

    def body(i, carry):
        r = pl.multiple_of(i * Q_TILE, Q_TILE)
        o0 = _softmax_pv(q_ref[0, 0, pl.ds(r, Q_TILE), :], k_ref[0, 0], v_ref[0, 0])
        o1 = _softmax_pv(q_ref[0, 1, pl.ds(r, Q_TILE), :], k_ref[0, 1], v_ref[0, 0])
        o_ref[0, pl.ds(r, Q_TILE), :] = jnp.where(lo, o0, o1).astype(BF16)
        return carry

    lax.fori_loop(0, seq // Q_TILE, body, 0)


def _mla_attn(q, k, v):
    B, _, S, _ = q.shape
    pair = pl.BlockSpec((1, 2, S, LANES), lambda b, p: (b, p, 0, 0))
    return pl.pallas_call(
        functools.partial(_mla_attn_kernel, seq=S),
        grid=(B, MLA_PAIRS),
        in_specs=[pair, pair, pl.BlockSpec((1, 1, S, LANES), lambda b, p: (b, p, 0, 0))],
        out_specs=pl.BlockSpec((1, S, LANES), lambda b, p: (b, 0, p)),
        out_shape=jax.ShapeDtypeStruct((B, S, MLA_HEADS * MLA_V), BF16),
        compiler_params=_params(),
        name="mla_attn",
    )(q, k, v)


def _out_mlp_kernel(*refs, n_parts, final):
    x_ref = refs[0]
    part_refs = refs[1:1 + n_parts]
    wo_ref, g_ref, wu_ref, wd_ref = refs[1 + n_parts:5 + n_parts]
    gf_ref = refs[5 + n_parts] if final else None
    o_ref = refs[-1]
    attn = jnp.concatenate([p_ref[0] for p_ref in part_refs], axis=-1)
    x = x_ref[0] + jnp.dot(attn, wo_ref[...], preferred_element_type=F32)
    h = _rms(x, g_ref[...]).astype(BF16)
    acc = x
    for c0 in range(0, D_FF, FF_CHUNK):
        u = jnp.dot(h, wu_ref[:, c0:c0 + FF_CHUNK], preferred_element_type=F32)
        u = jnp.square(jnp.maximum(u, 0.0)).astype(BF16)
        acc = acc + jnp.dot(u, wd_ref[c0:c0 + FF_CHUNK, :], preferred_element_type=F32)
    if final:
        acc = _rms(acc, gf_ref[...])
    o_ref[0] = acc


def _out_mlp(x, parts, w_out, idx, g, w_up, w_down, layer, final_g):
    B, S, _ = x.shape
    nt = S // ROW_TILE
    final = final_g is not None
    row = lambda width: pl.BlockSpec((1, ROW_TILE, width), lambda b, t: (b, t, 0))
    in_specs = [row(D_MODEL)] + [row(p.shape[-1]) for p in parts] + [
        pl.BlockSpec((None, D_MODEL, D_MODEL), lambda b, t: (idx, 0, 0)),
        pl.BlockSpec((None, 1, D_MODEL), lambda b, t: (layer, 0, 0)),
        pl.BlockSpec((None, D_MODEL, D_FF), lambda b, t: (layer, 0, 0)),
        pl.BlockSpec((None, D_FF, D_MODEL), lambda b, t: (layer, 0, 0)),
    ]
    args = [x, *parts, w_out, g, w_up, w_down]
    if final:
        in_specs.append(pl.BlockSpec((1, D_MODEL), lambda b, t: (0, 0)))
        args.append(final_g)
    return pl.pallas_call(
        functools.partial(_out_mlp_kernel, n_parts=len(parts), final=final),
        grid=(B, nt),
        in_specs=in_specs,
        out_specs=row(D_MODEL),
        out_shape=jax.ShapeDtypeStruct((B, S, D_MODEL), F32),
        compiler_params=_params(),
        name="out_mlp",
    )(*args)


def _mla_weight_layouts(od_w_in, od_w_uq, od_w_ukv):
    n = od_w_in.shape[0]
    pad_lo = jnp.zeros((n, D_MODEL, MLA_NOPE), F32)
    pad_hi = jnp.zeros((n, D_MODEL, LANES - MLA_NOPE - MLA_ROPE), F32)
    w_in = jnp.concatenate([od_w_in[:, :, :Q_LORA + KV_LORA], pad_lo, od_w_in[:, :, Q_LORA + KV_LORA:], pad_hi], axis=-1)
    w_uq = od_w_uq.reshape(n, Q_LORA, MLA_HEADS, MLA_NOPE + MLA_ROPE)
    w_uq = jnp.pad(w_uq, ((0, 0), (0, 0), (0, 0), (0, LANES - MLA_NOPE - MLA_ROPE))).reshape(n, Q_LORA, MLA_HEADS * LANES)
    w_ukv = od_w_ukv.reshape(n, KV_LORA, MLA_HEADS, MLA_NOPE + MLA_V)
    w_uk = jnp.pad(w_ukv[..., :MLA_NOPE], ((0, 0), (0, 0), (0, 0), (0, LANES - MLA_NOPE))).reshape(n, KV_LORA, MLA_HEADS * LANES)
    w_uv = w_ukv[..., MLA_NOPE:].reshape(n, KV_LORA, MLA_HEADS * MLA_V)
    return w_in.astype(BF16), w_uq.astype(BF16), w_uk.astype(BF16), w_uv.astype(BF16)


def kernel(x, ln_mix_g, ln_mlp_g, w_up, w_down, ln_f_g, ev_w_in, ev_w_out, ev_lambda_q1, ev_lambda_k1, ev_lambda_q2, ev_lambda_k2, ev_subln_g, ev_rpb, od_w_in, od_q_norm_g, od_kv_norm_g, od_w_uq, od_w_ukv, od_w_out):
    B, S, D = x.shape
    assert D == D_MODEL and S % ROW_TILE == 0 and S % Q_TILE == 0 and S % GRID_W == 0
    x = x.astype(F32)
    ln_mix = ln_mix_g.astype(F32)[:, None, :]
    ln_mlp = ln_mlp_g.astype(F32)[:, None, :]
    w_up_b, w_down_b = w_up.astype(BF16), w_down.astype(BF16)
    ev_w_in_b, ev_w_out_b, od_w_out_b = ev_w_in.astype(BF16), ev_w_out.astype(BF16), od_w_out.astype(BF16)
    od_w_in_b, od_w_uq_b, od_w_uk_b, od_w_uv_b = _mla_weight_layouts(od_w_in.astype(F32), od_w_uq.astype(F32), od_w_ukv.astype(F32))
    gq = od_q_norm_g.astype(F32)[:, None, :]
    gkv = od_kv_norm_g.astype(F32)[:, None, :]
    even_tables = _rope_tables(S, ROT_DIM, 0, HEAD_DIM)
    mla_tables = _rope_tables(S, MLA_ROPE, MLA_NOPE, LANES)
    final_g = ln_f_g.astype(F32)[None, :]

    for l in range(DEPTH):
        i = l // 2
        last = final_g if l == DEPTH - 1 else None
        if l % 2 == 0:
            lam_init = 0.8 - 0.6 * math.exp(-0.3 * l)
            proj = _even_proj(x, ln_mix, ev_w_in_b, even_tables, l, i)
            lam_params = jnp.stack([ev_lambda_q1[i], ev_lambda_k1[i], ev_lambda_q2[i], ev_lambda_k2[i]]).astype(F32)
            o_a = _diff_attn(proj, lam_params, ev_subln_g[i].astype(F32)[None, :], lam_init)
            o_b = _natten(proj, _natten_bias(ev_rpb[i]))
            parts, w_out, idx = [o_a, o_b], ev_w_out_b, i
        else:
            q, k, v = _mla_proj(x, ln_mix, od_w_in_b, gq, gkv, od_w_uq_b, od_w_uk_b, od_w_uv_b, mla_tables, l, i)
            parts, w_out, idx = [_mla_attn(q, k, v)], od_w_out_b, i
        x = _out_mlp(x, parts, w_out, idx, ln_mlp, w_up_b, w_down_b, l, last)
    return x
```

```python
import functools
import math

import jax
import jax.numpy as jnp
import numpy as np
from jax import lax
from jax.experimental import pallas as pl
from jax.experimental.pallas import tpu as pltpu

D_MODEL = 1024
DEPTH = 4
HEAD_DIM = 64
A_HEADS = D_MODEL // 256
A_V_DIM = 2 * HEAD_DIM
B_HEADS = D_MODEL // 128
ROT_DIM = HEAD_DIM // 4
ROPE_THETA = 500000.0
GRID_W = 64
WIN_R = 8
WIN_C = 16
MLA_HEADS = D_MODEL // 64
MLA_NOPE = 64
MLA_ROPE = 32
MLA_V = 64
Q_LORA = D_MODEL // 4
KV_LORA = D_MODEL // 8
D_FF = 4 * D_MODEL
EPS = 1e-5
A_QK = A_HEADS * 2 * HEAD_DIM
A_V = A_HEADS * A_V_DIM
B_W = B_HEADS * HEAD_DIM
EVEN_IN = 2 * A_QK + A_V + 3 * B_W

LANES = 128
EVEN_SLABS = EVEN_IN // LANES
MLA_PAIRS = MLA_HEADS // 2
VMEM_LIMIT_BYTES = 56 * 1024 * 1024

ROW_TILE = 512
FF_CHUNK = 1024
Q_TILE = 256

F32 = jnp.float32
BF16 = jnp.bfloat16


def _params():
    return pltpu.CompilerParams(vmem_limit_bytes=VMEM_LIMIT_BYTES)


def _rms(x, g):
    return x * lax.rsqrt(jnp.mean(x * x, axis=-1, keepdims=True) + EPS) * g


def _rope(x, c, sa, sb, half):
    return x * c + pltpu.roll(x, half, 1) * sa + pltpu.roll(x, LANES - half, 1) * sb


def _rope_tables(seq, rot_dim, lane_of_dim0, period):
    half = rot_dim // 2
    pos = np.arange(seq, dtype=np.float32)
    inv = (np.float32(ROPE_THETA) ** (-np.arange(0, rot_dim, 2, dtype=np.float32) / np.float32(rot_dim))).astype(np.float32)
    pos = jnp.asarray(pos)
    ang = pos[:, None] * jnp.asarray(inv)[None, :]
    cos, sin = jnp.cos(ang), jnp.sin(ang)
    c = jnp.ones((seq, period), F32)
    sa = jnp.zeros((seq, period), F32)
    sb = jnp.zeros((seq, period), F32)
    lo = lane_of_dim0
    c = c.at[:, lo:lo + half].set(cos).at[:, lo + half:lo + rot_dim].set(cos)
    sb = sb.at[:, lo:lo + half].set(-sin)
    sa = sa.at[:, lo + half:lo + rot_dim].set(sin)
    rep = LANES // period
    return tuple(jnp.tile(t, (1, rep)) for t in (c, sa, sb))


def _even_proj_kernel(x_ref, g_ref, w_ref, c_ref, sa_ref, sb_ref, o_ref):
    h = _rms(x_ref[0], g_ref[...]).astype(BF16)
    c, sa, sb = c_ref[...], sa_ref[...], sb_ref[...]
    group = 4
    for j0 in range(0, EVEN_SLABS, group):
        p = jnp.dot(h, w_ref[:, j0 * LANES:(j0 + group) * LANES], preferred_element_type=F32)
        for jj in range(group):
            j = j0 + jj
            blk = p[:, jj * LANES:(jj + 1) * LANES]
            if j < 2 * A_HEADS:
                blk = _rope(blk, c, sa, sb, ROT_DIM // 2)
            if j < A_HEADS or 3 * A_HEADS <= j < 4 * A_HEADS:
                blk = blk * (HEAD_DIM ** -0.5)
            o_ref[0, j] = blk.astype(BF16)


def _even_proj(x, g, w, tables, layer, idx):
    B, S, _ = x.shape
    nt = S // ROW_TILE
    tab_spec = pl.BlockSpec((ROW_TILE, LANES), lambda b, t: (t, 0))
    return pl.pallas_call(
        _even_proj_kernel,
        grid=(B, nt),
        in_specs=[
            pl.BlockSpec((1, ROW_TILE, D_MODEL), lambda b, t: (b, t, 0)),
            pl.BlockSpec((None, 1, D_MODEL), lambda b, t: (layer, 0, 0)),
            pl.BlockSpec((None, D_MODEL, EVEN_IN), lambda b, t: (idx, 0, 0)),
            tab_spec, tab_spec, tab_spec,
        ],
        out_specs=pl.BlockSpec((1, EVEN_SLABS, ROW_TILE, LANES), lambda b, t: (b, 0, t, 0)),
        out_shape=jax.ShapeDtypeStruct((B, EVEN_SLABS, S, LANES), BF16),
        compiler_params=_params(),
        name="even_proj",
    )(x, g, w, *tables)


def _softmax_pv(q, k, v):
    s = lax.dot_general(q, k, (((1,), (1,)), ((), ())), preferred_element_type=F32)
    m = jnp.max(s, axis=-1, keepdims=True)
    p = jnp.exp(s - m)
    l = jnp.sum(p, axis=-1, keepdims=True)
    return jnp.dot(p.astype(BF16), v, preferred_element_type=F32) / l


def _low_lanes(rows):
    return lax.broadcasted_iota(jnp.int32, (rows, LANES), 1) < HEAD_DIM


def _diff_attn_kernel(q_ref, k_ref, v_ref, lp_ref, g_ref, o_ref, *, lam_init, seq):
    lp = lp_ref[...]
    lam = (jnp.exp(jnp.sum(lp[0:1] * lp[1:2], axis=-1, keepdims=True))
           - jnp.exp(jnp.sum(lp[2:3] * lp[3:4], axis=-1, keepdims=True)) + lam_init)
    lo = _low_lanes(Q_TILE)
    gain = g_ref[...] * (1.0 - lam_init)

    def body(i, carry):
        r = pl.multiple_of(i * Q_TILE, Q_TILE)
        q = q_ref[0, 0, pl.ds(r, Q_TILE), :]
        zero = jnp.zeros_like(q)
        o1 = _softmax_pv(jnp.where(lo, q, zero), k_ref[0, 0], v_ref[0, 0])
        o2 = _softmax_pv(jnp.where(lo, zero, q), k_ref[0, 0], v_ref[0, 0])
        o = o1 - lam * o2
        o_ref[0, pl.ds(r, Q_TILE), :] = _rms(o, gain).astype(BF16)
        return carry

    lax.fori_loop(0, seq // Q_TILE, body, 0)


def _diff_attn(proj, lam_params, subln_g, lam_init):
    B, _, S, _ = proj.shape
    slab = lambda off: pl.BlockSpec((1, 1, S, LANES), lambda b, h: (b, off + h, 0, 0))
    return pl.pallas_call(
        functools.partial(_diff_attn_kernel, lam_init=lam_init, seq=S),
        grid=(B, A_HEADS),
        in_specs=[
            slab(0), slab(A_HEADS), slab(2 * A_HEADS),
            pl.BlockSpec((4, HEAD_DIM), lambda b, h: (0, 0)),
            pl.BlockSpec((1, A_V_DIM), lambda b, h: (0, 0)),
        ],
        out_specs=pl.BlockSpec((1, S, LANES), lambda b, h: (b, 0, h)),
        out_shape=jax.ShapeDtypeStruct((B, S, A_V), BF16),
        compiler_params=_params(),
        name="diff_attn",
    )(proj, proj, proj, lam_params, subln_g)


NAT_ROWS = 4
NAT_WIN = 12
NEG_BIG = -1e30


def _nat_window_start(g, rows):
    return jnp.clip(NAT_ROWS * g - WIN_R // 2, 0, rows - NAT_WIN)


def _natten_bias(rpb, rows):
    heads = rpb.shape[0]
    groups = rows // NAT_ROWS
    c = np.arange(GRID_W)
    cs = np.clip(c - WIN_C // 2, 0, GRID_W - WIN_C)
    col_mask = (c[None, :] >= cs[:, None]) & (c[None, :] < cs[:, None] + WIN_C)
    col_idx = np.clip(c[None, :] - c[:, None], -(WIN_C - 1), WIN_C - 1) + (WIN_C - 1)
    onehot = (col_idx[None] == np.arange(2 * WIN_C - 1)[:, None, None]).astype(np.float32)
    colbias = jnp.einsum("hrj,jqk->hrqk", rpb.astype(F32), jnp.asarray(onehot), precision=lax.Precision.HIGHEST)
    colbias = jnp.where(jnp.asarray(col_mask)[None, None], colbias, NEG_BIG)
    neg = jnp.full((heads, GRID_W, GRID_W), NEG_BIG, F32)
    variants = []
    for g in (0, 1, groups - 1):
        win0 = int(np.clip(NAT_ROWS * g - WIN_R // 2, 0, rows - NAT_WIN))
        per_row = []
        for a in range(NAT_ROWS):
            r = NAT_ROWS * g + a
            r0 = int(np.clip(r - WIN_R // 2, 0, rows - WIN_R))
            blocks = [colbias[:, kr - r + WIN_R - 1] if r0 <= kr < r0 + WIN_R else neg
                      for kr in range(win0, win0 + NAT_WIN)]
            per_row.append(jnp.stack(blocks, axis=2))
        variants.append(jnp.stack(per_row, axis=1))
    return jnp.stack(variants, axis=1).reshape(heads, 3, NAT_ROWS * GRID_W, NAT_WIN * GRID_W)


def _natten_kernel(q_ref, k_ref, v_ref, b_ref, o_ref, *, rows):
    qt = NAT_ROWS * GRID_W
    win = NAT_WIN * GRID_W
    lo = _low_lanes(qt)

    def body(g, carry):
        win0 = _nat_window_start(g, rows)
        variant = lax.shift_right_logical(NAT_ROWS * g - win0, int(math.log2(NAT_ROWS)))
        qstart = pl.multiple_of(g * qt, qt)
        kstart = pl.multiple_of(win0 * GRID_W, GRID_W)
        q = q_ref[0, 0, pl.ds(qstart, qt), :]
        ks = k_ref[0, 0, pl.ds(kstart, win), :]
        vs = v_ref[0, 0, pl.ds(kstart, win), :]
        zero = jnp.zeros_like(q)
        outs = []
        for hd in range(2):
            qm = jnp.where(lo, q, zero) if hd == 0 else jnp.where(lo, zero, q)
            s = lax.dot_general(qm, ks, (((1,), (1,)), ((), ())), preferred_element_type=F32)
            s = s + b_ref[hd, variant]
            m = jnp.max(s, axis=-1, keepdims=True)
            p = jnp.exp(s - m)
            l = jnp.sum(p, axis=-1, keepdims=True)
            outs.append(jnp.dot(p.astype(BF16), vs, preferred_element_type=F32) / l)
        o_ref[0, pl.ds(qstart, qt), :] = jnp.where(lo, outs[0], outs[1]).astype(BF16)
        return carry

    lax.fori_loop(0, rows // NAT_ROWS, body, 0)


def _natten(proj, bias):
    B, _, S, _ = proj.shape
    rows = S // GRID_W
    assert NAT_ROWS == WIN_R // 2 and rows % NAT_ROWS == 0 and rows >= NAT_WIN
    base = 3 * A_HEADS
    pairs = B_HEADS // 2
    slab = lambda off: pl.BlockSpec((1, 1, S, LANES), lambda p, b: (b, off + p, 0, 0))
    return pl.pallas_call(
        functools.partial(_natten_kernel, rows=rows),
        grid=(pairs, B),
        in_specs=[
            slab(base), slab(base + pairs), slab(base + 2 * pairs),
            pl.BlockSpec((2, 3, NAT_ROWS * GRID_W, NAT_WIN * GRID_W), lambda p, b: (p, 0, 0, 0)),
        ],
        out_specs=pl.BlockSpec((1, S, LANES), lambda p, b: (b, 0, p)),
        out_shape=jax.ShapeDtypeStruct((B, S, B_W), BF16),
        compiler_params=_params(),
        name="natten",
    )(proj, proj, proj, bias)


def _mla_proj_kernel(x_ref, g_ref, win_ref, gq_ref, gkv_ref, wuq_ref, wuk_ref, wuv_ref,
                     c_ref, sa_ref, sb_ref, q_ref, k_ref, v_ref):
    h = _rms(x_ref[0], g_ref[...]).astype(BF16)
    c, sa, sb = c_ref[...], sa_ref[...], sb_ref[...]
    lat = jnp.dot(h, win_ref[...], preferred_element_type=F32)
    cq = _rms(lat[:, :Q_LORA], gq_ref[...]).astype(BF16)
    ckv = _rms(lat[:, Q_LORA:Q_LORA + KV_LORA], gkv_ref[...]).astype(BF16)
    k_rope = _rope(lat[:, Q_LORA + KV_LORA:], c, sa, sb, MLA_ROPE // 2)
    scale = (MLA_NOPE + MLA_ROPE) ** -0.5
    group = 4
    for j0 in range(0, MLA_HEADS, group):
        cols = slice(j0 * LANES, (j0 + group) * LANES)
        qg = jnp.dot(cq, wuq_ref[:, cols], preferred_element_type=F32)
        kg = jnp.dot(ckv, wuk_ref[:, cols], preferred_element_type=F32)
        for jj in range(group):
            sl = slice(jj * LANES, (jj + 1) * LANES)
            q_ref[0, j0 + jj] = (_rope(qg[:, sl], c, sa, sb, MLA_ROPE // 2) * scale).astype(BF16)
            k_ref[0, j0 + jj] = (kg[:, sl] + k_rope).astype(BF16)
    vv = jnp.dot(ckv, wuv_ref[...], preferred_element_type=F32)
    for j in range(MLA_PAIRS):
        v_ref[0, j] = vv[:, j * LANES:(j + 1) * LANES].astype(BF16)


def _mla_proj(x, g, w_in, gq, gkv, w_uq, w_uk, w_uv, tables, layer, idx):
    B, S, _ = x.shape
    nt = S // ROW_TILE
    tab_spec = pl.BlockSpec((ROW_TILE, LANES), lambda b, t: (t, 0))
    whole = lambda a: pl.BlockSpec((None,) + a.shape[1:], lambda b, t: (idx,) + (0,) * (a.ndim - 1))
    head_out = lambda n: pl.BlockSpec((1, n, ROW_TILE, LANES), lambda b, t: (b, 0, t, 0))
    return pl.pallas_call(
        _mla_proj_kernel,
        grid=(B, nt),
        in_specs=[
            pl.BlockSpec((1, ROW_TILE, D_MODEL), lambda b, t: (b, t, 0)),
            pl.BlockSpec((None, 1, D_MODEL), lambda b, t: (layer, 0, 0)),
            whole(w_in), whole(gq), whole(gkv), whole(w_uq), whole(w_uk), whole(w_uv),
            tab_spec, tab_spec, tab_spec,
        ],
        out_specs=[head_out(MLA_HEADS), head_out(MLA_HEADS), head_out(MLA_PAIRS)],
        out_shape=[
            jax.ShapeDtypeStruct((B, MLA_HEADS, S, LANES), BF16),
            jax.ShapeDtypeStruct((B, MLA_HEADS, S, LANES), BF16),
            jax.ShapeDtypeStruct((B, MLA_PAIRS, S, LANES), BF16),
        ],
        compiler_params=_params(),
        name="mla_proj",
    )(x, g, w_in, gq, gkv, w_uq, w_uk, w_uv, *tables)


def _mla_attn_kernel(q_ref, k_ref, v_ref, o_ref, *, seq):
    lo = _low_lanes(Q_TILE)

    def body(i, carry):
        r = pl.multiple_of(i * Q_TILE, Q_TILE)
        o0 = _softmax_pv(q_ref[0, 0, pl.ds(r, Q_TILE), :], k_ref[0, 0], v_ref[0, 0])
        o1 = _softmax_pv(q_ref[0, 1, pl.ds(r, Q_TILE), :], k_ref[0, 1], v_ref[0, 0])
        o_ref[0, pl.ds(r, Q_TILE), :] = jnp.where(lo, o0, o1).astype(BF16)
        return carry

    lax.fori_loop(0, seq // Q_TILE, body, 0)


def _mla_attn(q, k, v):
    B, _, S, _ = q.shape
    pair = pl.BlockSpec((1, 2, S, LANES), lambda b, p: (b, p, 0, 0))
    return pl.pallas_call(
        functools.partial(_mla_attn_kernel, seq=S),
        grid=(B, MLA_PAIRS),
        in_specs=[pair, pair, pl.BlockSpec((1, 1, S, LANES), lambda b, p: (b, p, 0, 0))],
        out_specs=pl.BlockSpec((1, S, LANES), lambda b, p: (b, 0, p)),
        out_shape=jax.ShapeDtypeStruct((B, S, MLA_HEADS * MLA_V), BF16),
        compiler_params=_params(),
        name="mla_attn",
    )(q, k, v)


def _out_mlp_kernel(*refs, n_parts, final):
    x_ref = refs[0]
    part_refs = refs[1:1 + n_parts]
    wo_ref, g_ref, wu_ref, wd_ref = refs[1 + n_parts:5 + n_parts]
    gf_ref = refs[5 + n_parts] if final else None
    o_ref = refs[-1]
    attn = jnp.concatenate([p_ref[0] for p_ref in part_refs], axis=-1)
    x = x_ref[0] + jnp.dot(attn, wo_ref[...], preferred_element_type=F32)
    h = _rms(x, g_ref[...]).astype(BF16)
    acc = x
    for c0 in range(0, D_FF, FF_CHUNK):
        u = jnp.dot(h, wu_ref[:, c0:c0 + FF_CHUNK], preferred_element_type=F32)
        u = jnp.square(jnp.maximum(u, 0.0)).astype(BF16)
        acc = acc + jnp.dot(u, wd_ref[c0:c0 + FF_CHUNK, :], preferred_element_type=F32)
    if final:
        acc = _rms(acc, gf_ref[...])
    o_ref[0] = acc


def _out_mlp(x, parts, w_out, idx, g, w_up, w_down, layer, final_g):
    B, S, _ = x.shape
    nt = S // ROW_TILE
    final = final_g is not None
    row = lambda width: pl.BlockSpec((1, ROW_TILE, width), lambda b, t: (b, t, 0))
    in_specs = [row(D_MODEL)] + [row(p.shape[-1]) for p in parts] + [
        pl.BlockSpec((None, D_MODEL, D_MODEL), lambda b, t: (idx, 0, 0)),
        pl.BlockSpec((None, 1, D_MODEL), lambda b, t: (layer, 0, 0)),
        pl.BlockSpec((None, D_MODEL, D_FF), lambda b, t: (layer, 0, 0)),
        pl.BlockSpec((None, D_FF, D_MODEL), lambda b, t: (layer, 0, 0)),
    ]
    args = [x, *parts, w_out, g, w_up, w_down]
    if final:
        in_specs.append(pl.BlockSpec((1, D_MODEL), lambda b, t: (0, 0)))
        args.append(final_g)
    return pl.pallas_call(
        functools.partial(_out_mlp_kernel, n_parts=len(parts), final=final),
        grid=(B, nt),
        in_specs=in_specs,
        out_specs=row(D_MODEL),
        out_shape=jax.ShapeDtypeStruct((B, S, D_MODEL), F32),
        compiler_params=_params(),
        name="out_mlp",
    )(*args)


def _mla_weight_layouts(od_w_in, od_w_uq, od_w_ukv):
    n = od_w_in.shape[0]
    pad_lo = jnp.zeros((n, D_MODEL, MLA_NOPE), F32)
    pad_hi = jnp.zeros((n, D_MODEL, LANES - MLA_NOPE - MLA_ROPE), F32)
    w_in = jnp.concatenate([od_w_in[:, :, :Q_LORA + KV_LORA], pad_lo, od_w_in[:, :, Q_LORA + KV_LORA:], pad_hi], axis=-1)
    w_uq = od_w_uq.reshape(n, Q_LORA, MLA_HEADS, MLA_NOPE + MLA_ROPE)
    w_uq = jnp.pad(w_uq, ((0, 0), (0, 0), (0, 0), (0, LANES - MLA_NOPE - MLA_ROPE))).reshape(n, Q_LORA, MLA_HEADS * LANES)
    w_ukv = od_w_ukv.reshape(n, KV_LORA, MLA_HEADS, MLA_NOPE + MLA_V)
    w_uk = jnp.pad(w_ukv[..., :MLA_NOPE], ((0, 0), (0, 0), (0, 0), (0, LANES - MLA_NOPE))).reshape(n, KV_LORA, MLA_HEADS * LANES)
    w_uv = w_ukv[..., MLA_NOPE:].reshape(n, KV_LORA, MLA_HEADS * MLA_V)
    return w_in.astype(BF16), w_uq.astype(BF16), w_uk.astype(BF16), w_uv.astype(BF16)


def kernel(x, ln_mix_g, ln_mlp_g, w_up, w_down, ln_f_g, ev_w_in, ev_w_out, ev_lambda_q1, ev_lambda_k1, ev_lambda_q2, ev_lambda_k2, ev_subln_g, ev_rpb, od_w_in, od_q_norm_g, od_kv_norm_g, od_w_uq, od_w_ukv, od_w_out):
    B, S, D = x.shape
    assert D == D_MODEL and S % ROW_TILE == 0 and S % Q_TILE == 0 and S % GRID_W == 0
    x = x.astype(F32)
    ln_mix = ln_mix_g.astype(F32)[:, None, :]
    ln_mlp = ln_mlp_g.astype(F32)[:, None, :]
    w_up_b, w_down_b = w_up.astype(BF16), w_down.astype(BF16)
    ev_w_in_b, ev_w_out_b, od_w_out_b = ev_w_in.astype(BF16), ev_w_out.astype(BF16), od_w_out.astype(BF16)
    od_w_in_b, od_w_uq_b, od_w_uk_b, od_w_uv_b = _mla_weight_layouts(od_w_in.astype(F32), od_w_uq.astype(F32), od_w_ukv.astype(F32))
    gq = od_q_norm_g.astype(F32)[:, None, :]
    gkv = od_kv_norm_g.astype(F32)[:, None, :]
    even_tables = _rope_tables(S, ROT_DIM, 0, HEAD_DIM)
    mla_tables = _rope_tables(S, MLA_ROPE, MLA_NOPE, LANES)
    final_g = ln_f_g.astype(F32)[None, :]

    for l in range(DEPTH):
        i = l // 2
        last = final_g if l == DEPTH - 1 else None
        if l % 2 == 0:
            lam_init = 0.8 - 0.6 * math.exp(-0.3 * l)
            proj = _even_proj(x, ln_mix, ev_w_in_b, even_tables, l, i)
            lam_params = jnp.stack([ev_lambda_q1[i], ev_lambda_k1[i], ev_lambda_q2[i], ev_lambda_k2[i]]).astype(F32)
            o_a = _diff_attn(proj, lam_params, ev_subln_g[i].astype(F32)[None, :], lam_init)
            o_b = _natten(proj, _natten_bias(ev_rpb[i], S // GRID_W))
            parts, w_out, idx = [o_a, o_b], ev_w_out_b, i
        else:
            q, k, v = _mla_proj(x, ln_mix, od_w_in_b, gq, gkv, od_w_uq_b, od_w_uk_b, od_w_uv_b, mla_tables, l, i)
            parts, w_out, idx = [_mla_attn(q, k, v)], od_w_out_b, i
        x = _out_mlp(x, parts, w_out, idx, ln_mlp, w_up_b, w_down_b, l, last)
    return x
```

```python
import functools
import math

import jax
import jax.numpy as jnp
import numpy as np
from jax import lax
from jax.experimental import pallas as pl
from jax.experimental.pallas import tpu as pltpu

D_MODEL = 1024
DEPTH = 4
HEAD_DIM = 64
A_HEADS = D_MODEL // 256
A_V_DIM = 2 * HEAD_DIM
B_HEADS = D_MODEL // 128
ROT_DIM = HEAD_DIM // 4
ROPE_THETA = 500000.0
GRID_W = 64
WIN_R = 8
WIN_C = 16
MLA_HEADS = D_MODEL // 64
MLA_NOPE = 64
MLA_ROPE = 32
MLA_V = 64
Q_LORA = D_MODEL // 4
KV_LORA = D_MODEL // 8
D_FF = 4 * D_MODEL
EPS = 1e-5
A_QK = A_HEADS * 2 * HEAD_DIM
A_V = A_HEADS * A_V_DIM
B_W = B_HEADS * HEAD_DIM
EVEN_IN = 2 * A_QK + A_V + 3 * B_W

LANES = 128
EVEN_SLABS = EVEN_IN // LANES
MLA_PAIRS = MLA_HEADS // 2
VMEM_LIMIT_BYTES = 56 * 1024 * 1024

ROW_TILE = 512
FF_CHUNK = 1024
Q_TILE = 256

F32 = jnp.float32
BF16 = jnp.bfloat16
LOG2E = math.log2(math.e)


def _params():
    return pltpu.CompilerParams(vmem_limit_bytes=VMEM_LIMIT_BYTES)


def _rms(x, g):
    return x * lax.rsqrt(jnp.mean(x * x, axis=-1, keepdims=True) + EPS) * g


def _rope(x, c, sa, sb, half):
    return x * c + pltpu.roll(x, half, 1) * sa + pltpu.roll(x, LANES - half, 1) * sb


def _rope_tables(seq, rot_dim, lane_of_dim0, period):
    half = rot_dim // 2
    pos = np.arange(seq, dtype=np.float32)
    inv = (np.float32(ROPE_THETA) ** (-np.arange(0, rot_dim, 2, dtype=np.float32) / np.float32(rot_dim))).astype(np.float32)
    pos = jnp.asarray(pos)
    ang = pos[:, None] * jnp.asarray(inv)[None, :]
    cos, sin = jnp.cos(ang), jnp.sin(ang)
    c = jnp.ones((seq, period), F32)
    sa = jnp.zeros((seq, period), F32)
    sb = jnp.zeros((seq, period), F32)
    lo = lane_of_dim0
    c = c.at[:, lo:lo + half].set(cos).at[:, lo + half:lo + rot_dim].set(cos)
    sb = sb.at[:, lo:lo + half].set(-sin)
    sa = sa.at[:, lo + half:lo + rot_dim].set(sin)
    rep = LANES // period
    return tuple(jnp.tile(t, (1, rep)) for t in (c, sa, sb))


def _even_proj_kernel(x_ref, g_ref, w_ref, c_ref, sa_ref, sb_ref, o_ref):
    h = _rms(x_ref[0], g_ref[...]).astype(BF16)
    c, sa, sb = c_ref[...], sa_ref[...], sb_ref[...]
    group = 4
    for j0 in range(0, EVEN_SLABS, group):
        p = jnp.dot(h, w_ref[:, j0 * LANES:(j0 + group) * LANES], preferred_element_type=F32)
        for jj in range(group):
            j = j0 + jj
            blk = p[:, jj * LANES:(jj + 1) * LANES]
            if j < 2 * A_HEADS:
                blk = _rope(blk, c, sa, sb, ROT_DIM // 2)
            if j < A_HEADS or 3 * A_HEADS <= j < 4 * A_HEADS:
                blk = blk * (HEAD_DIM ** -0.5 * LOG2E)
            o_ref[0, j] = blk.astype(BF16)


def _even_proj(x, g, w, tables, layer, idx):
    B, S, _ = x.shape
    nt = S // ROW_TILE
    tab_spec = pl.BlockSpec((ROW_TILE, LANES), lambda b, t: (t, 0))
    return pl.pallas_call(
        _even_proj_kernel,
        grid=(B, nt),
        in_specs=[
            pl.BlockSpec((1, ROW_TILE, D_MODEL), lambda b, t: (b, t, 0)),
            pl.BlockSpec((None, 1, D_MODEL), lambda b, t: (layer, 0, 0)),
            pl.BlockSpec((None, D_MODEL, EVEN_IN), lambda b, t: (idx, 0, 0)),
            tab_spec, tab_spec, tab_spec,
        ],
        out_specs=pl.BlockSpec((1, EVEN_SLABS, ROW_TILE, LANES), lambda b, t: (b, 0, t, 0)),
        out_shape=jax.ShapeDtypeStruct((B, EVEN_SLABS, S, LANES), BF16),
        compiler_params=_params(),
        name="even_proj",
    )(x, g, w, *tables)


def _softmax_rows(s):
    return jnp.exp2(s - jnp.max(s, axis=-1, keepdims=True)).astype(BF16)


def _scores(q, k):
    return lax.dot_general(q, k, (((1,), (1,)), ((), ())), preferred_element_type=F32)


def _low_lanes(rows):
    return lax.broadcasted_iota(jnp.int32, (rows, LANES), 1) < HEAD_DIM


def _stage_pair_values(v, vx_ref):
    lo = _low_lanes(v.shape[0])
    ones = jnp.ones_like(v)
    vx_ref[0] = jnp.where(lo, v, ones)
    vx_ref[1] = jnp.where(lo, ones, v)


def _pair_output(res_a, res_b, lo):
    num = jnp.where(lo, res_a, res_b)
    den = pltpu.roll(jnp.where(lo, res_b, res_a), HEAD_DIM, 1)
    return num / den


def _diff_attn_kernel(q_ref, k_ref, v_ref, lp_ref, g_ref, o_ref, *, lam_init, seq):
    lp = lp_ref[...]
    lam = (jnp.exp(jnp.sum(lp[0:1] * lp[1:2], axis=-1, keepdims=True))
           - jnp.exp(jnp.sum(lp[2:3] * lp[3:4], axis=-1, keepdims=True)) + lam_init)
    lo = _low_lanes(Q_TILE)
    gain = g_ref[...] * (1.0 - lam_init)

    def softmax_pv(q):
        s = _scores(q, k_ref[0, 0])
        p = jnp.exp2(s - jnp.max(s, axis=-1, keepdims=True))
        l = jnp.sum(p, axis=-1, keepdims=True)
        return jnp.dot(p.astype(BF16), v_ref[0, 0], preferred_element_type=F32) / l

    def body(i, carry):
        r = pl.multiple_of(i * Q_TILE, Q_TILE)
        q = q_ref[0, 0, pl.ds(r, Q_TILE), :]
        zero = jnp.zeros_like(q)
        o = softmax_pv(jnp.where(lo, q, zero)) - lam * softmax_pv(jnp.where(lo, zero, q))
        o_ref[0, pl.ds(r, Q_TILE), :] = _rms(o, gain).astype(BF16)
        return carry

    lax.fori_loop(0, seq // Q_TILE, body, 0, unroll=True)


def _diff_attn(proj, lam_params, subln_g, lam_init):
    B, _, S, _ = proj.shape
    slab = lambda off: pl.BlockSpec((1, 1, S, LANES), lambda b, h: (b, off + h, 0, 0))
    return pl.pallas_call(
        functools.partial(_diff_attn_kernel, lam_init=lam_init, seq=S),
        grid=(B, A_HEADS),
        in_specs=[
            slab(0), slab(A_HEADS), slab(2 * A_HEADS),
            pl.BlockSpec((4, HEAD_DIM), lambda b, h: (0, 0)),
            pl.BlockSpec((1, A_V_DIM), lambda b, h: (0, 0)),
        ],
        out_specs=pl.BlockSpec((1, S, LANES), lambda b, h: (b, 0, h)),
        out_shape=jax.ShapeDtypeStruct((B, S, A_V), BF16),
        compiler_params=_params(),
        name="diff_attn",
    )(proj, proj, proj, lam_params, subln_g)


NAT_ROWS = 4
NAT_WIN = 12
NEG_BIG = -1e30


def _nat_window_start(g, rows):
    return jnp.clip(NAT_ROWS * g - WIN_R // 2, 0, rows - NAT_WIN)


def _natten_bias(rpb, rows):
    heads = rpb.shape[0]
    groups = rows // NAT_ROWS
    c = np.arange(GRID_W)
    cs = np.clip(c - WIN_C // 2, 0, GRID_W - WIN_C)
    col_mask = (c[None, :] >= cs[:, None]) & (c[None, :] < cs[:, None] + WIN_C)
    col_idx = np.clip(c[None, :] - c[:, None], -(WIN_C - 1), WIN_C - 1) + (WIN_C - 1)
    onehot = (col_idx[None] == np.arange(2 * WIN_C - 1)[:, None, None]).astype(np.float32)
    colbias = jnp.einsum("hrj,jqk->hrqk", rpb.astype(F32), jnp.asarray(onehot), precision=lax.Precision.HIGHEST)
    colbias = jnp.where(jnp.asarray(col_mask)[None, None], colbias * LOG2E, NEG_BIG)
    neg = jnp.full((heads, GRID_W, GRID_W), NEG_BIG, F32)
    variants = []
    for g in (0, 1, groups - 1):
        win0 = int(np.clip(NAT_ROWS * g - WIN_R // 2, 0, rows - NAT_WIN))
        per_row = []
        for a in range(NAT_ROWS):
            r = NAT_ROWS * g + a
            r0 = int(np.clip(r - WIN_R // 2, 0, rows - WIN_R))
            blocks = [colbias[:, kr - r + WIN_R - 1] if r0 <= kr < r0 + WIN_R else neg
                      for kr in range(win0, win0 + NAT_WIN)]
            per_row.append(jnp.stack(blocks, axis=2))
        variants.append(jnp.stack(per_row, axis=1))
    return jnp.stack(variants, axis=1).reshape(heads, 3, NAT_ROWS * GRID_W, NAT_WIN * GRID_W)


def _natten_kernel(q_ref, k_ref, v_ref, b_ref, o_ref, vx_ref, *, rows):
    _stage_pair_values(v_ref[0, 0], vx_ref)
    qt = NAT_ROWS * GRID_W
    win = NAT_WIN * GRID_W
    lo = _low_lanes(qt)

    def body(g, carry):
        win0 = _nat_window_start(g, rows)
        variant = lax.shift_right_logical(NAT_ROWS * g - win0, int(math.log2(NAT_ROWS)))
        qstart = pl.multiple_of(g * qt, qt)
        kstart = pl.multiple_of(win0 * GRID_W, GRID_W)
        q = q_ref[0, 0, pl.ds(qstart, qt), :]
        ks = k_ref[0, 0, pl.ds(kstart, win), :]
        zero = jnp.zeros_like(q)
        res = []
        for hd in range(2):
            qm = jnp.where(lo, q, zero) if hd == 0 else jnp.where(lo, zero, q)
            p = _softmax_rows(_scores(qm, ks) + b_ref[hd, variant])
            res.append(jnp.dot(p, vx_ref[hd, pl.ds(kstart, win), :], preferred_element_type=F32))
        o_ref[0, pl.ds(qstart, qt), :] = _pair_output(res[0], res[1], lo).astype(BF16)
        return carry

    lax.fori_loop(0, rows // NAT_ROWS, body, 0)


def _natten(proj, bias):
    B, _, S, _ = proj.shape
    rows = S // GRID_W
    assert NAT_ROWS == WIN_R // 2 and rows % NAT_ROWS == 0 and rows >= NAT_WIN
    base = 3 * A_HEADS
    pairs = B_HEADS // 2
    slab = lambda off: pl.BlockSpec((1, 1, S, LANES), lambda p, b: (b, off + p, 0, 0))
    return pl.pallas_call(
        functools.partial(_natten_kernel, rows=rows),
        grid=(pairs, B),
        in_specs=[
            slab(base), slab(base + pairs), slab(base + 2 * pairs),
            pl.BlockSpec((2, 3, NAT_ROWS * GRID_W, NAT_WIN * GRID_W), lambda p, b: (p, 0, 0, 0)),
        ],
        out_specs=pl.BlockSpec((1, S, LANES), lambda p, b: (b, 0, p)),
        out_shape=jax.ShapeDtypeStruct((B, S, B_W), BF16),
        scratch_shapes=[pltpu.VMEM((2, S, LANES), BF16)],
        compiler_params=_params(),
        name="natten",
    )(proj, proj, proj, bias)


def _mla_proj_kernel(x_ref, g_ref, win_ref, gq_ref, gkv_ref, wuq_ref, wuk_ref, wuv_ref,
                     c_ref, sa_ref, sb_ref, q_ref, k_ref, v_ref):
    h = _rms(x_ref[0], g_ref[...]).astype(BF16)
    c, sa, sb = c_ref[...], sa_ref[...], sb_ref[...]
    lat = jnp.dot(h, win_ref[...], preferred_element_type=F32)
    cq = _rms(lat[:, :Q_LORA], gq_ref[...]).astype(BF16)
    ckv = _rms(lat[:, Q_LORA:Q_LORA + KV_LORA], gkv_ref[...]).astype(BF16)
    k_rope = _rope(lat[:, Q_LORA + KV_LORA:], c, sa, sb, MLA_ROPE // 2)
    scale = (MLA_NOPE + MLA_ROPE) ** -0.5 * LOG2E
    group = 4
    for j0 in range(0, MLA_HEADS, group):
        cols = slice(j0 * LANES, (j0 + group) * LANES)
        qg = jnp.dot(cq, wuq_ref[:, cols], preferred_element_type=F32)
        kg = jnp.dot(ckv, wuk_ref[:, cols], preferred_element_type=F32)
        for jj in range(group):
            sl = slice(jj * LANES, (jj + 1) * LANES)
            q_ref[0, j0 + jj] = (_rope(qg[:, sl], c, sa, sb, MLA_ROPE // 2) * scale).astype(BF16)
            k_ref[0, j0 + jj] = (kg[:, sl] + k_rope).astype(BF16)
    vv = jnp.dot(ckv, wuv_ref[...], preferred_element_type=F32)
    for j in range(MLA_PAIRS):
        v_ref[0, j] = vv[:, j * LANES:(j + 1) * LANES].astype(BF16)


def _mla_proj(x, g, w_in, gq, gkv, w_uq, w_uk, w_uv, tables, layer, idx):
    B, S, _ = x.shape
    nt = S // ROW_TILE
    tab_spec = pl.BlockSpec((ROW_TILE, LANES), lambda b, t: (t, 0))
    whole = lambda a: pl.BlockSpec((None,) + a.shape[1:], lambda b, t: (idx,) + (0,) * (a.ndim - 1))
    head_out = lambda n: pl.BlockSpec((1, n, ROW_TILE, LANES), lambda b, t: (b, 0, t, 0))
    return pl.pallas_call(
        _mla_proj_kernel,
        grid=(B, nt),
        in_specs=[
            pl.BlockSpec((1, ROW_TILE, D_MODEL), lambda b, t: (b, t, 0)),
            pl.BlockSpec((None, 1, D_MODEL), lambda b, t: (layer, 0, 0)),
            whole(w_in), whole(gq), whole(gkv), whole(w_uq), whole(w_uk), whole(w_uv),
            tab_spec, tab_spec, tab_spec,
        ],
        out_specs=[head_out(MLA_HEADS), head_out(MLA_HEADS), head_out(MLA_PAIRS)],
        out_shape=[
            jax.ShapeDtypeStruct((B, MLA_HEADS, S, LANES), BF16),
            jax.ShapeDtypeStruct((B, MLA_HEADS, S, LANES), BF16),
            jax.ShapeDtypeStruct((B, MLA_PAIRS, S, LANES), BF16),
        ],
        compiler_params=_params(),
        name="mla_proj",
    )(x, g, w_in, gq, gkv, w_uq, w_uk, w_uv, *tables)


def _mla_attn_kernel(q_ref, k_ref, v_ref, o_ref, vx_ref, *, seq):
    _stage_pair_values(v_ref[0, 0], vx_ref)
    lo = _low_lanes(Q_TILE)

    def body(i, carry):
        r = pl.multiple_of(i * Q_TILE, Q_TILE)
        res = []
        for hd in range(2):
            p = _softmax_rows(_scores(q_ref[0, hd, pl.ds(r, Q_TILE), :], k_ref[0, hd]))
            res.append(jnp.dot(p, vx_ref[hd], preferred_element_type=F32))
        o_ref[0, pl.ds(r, Q_TILE), :] = _pair_output(res[0], res[1], lo).astype(BF16)
        return carry

    lax.fori_loop(0, seq // Q_TILE, body, 0, unroll=True)


def _mla_attn(q, k, v):
    B, _, S, _ = q.shape
    pair = pl.BlockSpec((1, 2, S, LANES), lambda b, p: (b, p, 0, 0))
    return pl.pallas_call(
        functools.partial(_mla_attn_kernel, seq=S),
        grid=(B, MLA_PAIRS),
        in_specs=[pair, pair, pl.BlockSpec((1, 1, S, LANES), lambda b, p: (b, p, 0, 0))],
        out_specs=pl.BlockSpec((1, S, LANES), lambda b, p: (b, 0, p)),
        out_shape=jax.ShapeDtypeStruct((B, S, MLA_HEADS * MLA_V), BF16),
        scratch_shapes=[pltpu.VMEM((2, S, LANES), BF16)],
        compiler_params=_params(),
        name="mla_attn",
    )(q, k, v)


def _out_mlp_kernel(*refs, n_parts, final):
    x_ref = refs[0]
    part_refs = refs[1:1 + n_parts]
    wo_ref, g_ref, wu_ref, wd_ref = refs[1 + n_parts:5 + n_parts]
    gf_ref = refs[5 + n_parts] if final else None
    o_ref = refs[-1]
    attn = jnp.concatenate([p_ref[0] for p_ref in part_refs], axis=-1)
    x = x_ref[0] + jnp.dot(attn, wo_ref[...], preferred_element_type=F32)
    h = _rms(x, g_ref[...]).astype(BF16)
    acc = x
    for c0 in range(0, D_FF, FF_CHUNK):
        u = jnp.dot(h, wu_ref[:, c0:c0 + FF_CHUNK], preferred_element_type=F32)
        u = jnp.square(jnp.maximum(u, 0.0)).astype(BF16)
        acc = acc + jnp.dot(u, wd_ref[c0:c0 + FF_CHUNK, :], preferred_element_type=F32)
    if final:
        acc = _rms(acc, gf_ref[...])
    o_ref[0] = acc


def _out_mlp(x, parts, w_out, idx, g, w_up, w_down, layer, final_g):
    B, S, _ = x.shape
    nt = S // ROW_TILE
    final = final_g is not None
    row = lambda width: pl.BlockSpec((1, ROW_TILE, width), lambda b, t: (b, t, 0))
    in_specs = [row(D_MODEL)] + [row(p.shape[-1]) for p in parts] + [
        pl.BlockSpec((None, D_MODEL, D_MODEL), lambda b, t: (idx, 0, 0)),
        pl.BlockSpec((None, 1, D_MODEL), lambda b, t: (layer, 0, 0)),
        pl.BlockSpec((None, D_MODEL, D_FF), lambda b, t: (layer, 0, 0)),
        pl.BlockSpec((None, D_FF, D_MODEL), lambda b, t: (layer, 0, 0)),
    ]
    args = [x, *parts, w_out, g, w_up, w_down]
    if final:
        in_specs.append(pl.BlockSpec((1, D_MODEL), lambda b, t: (0, 0)))
        args.append(final_g)
    return pl.pallas_call(
        functools.partial(_out_mlp_kernel, n_parts=len(parts), final=final),
        grid=(B, nt),
        in_specs=in_specs,
        out_specs=row(D_MODEL),
        out_shape=jax.ShapeDtypeStruct((B, S, D_MODEL), F32),
        compiler_params=_params(),
        name="out_mlp",
    )(*args)


def _mla_weight_layouts(od_w_in, od_w_uq, od_w_ukv):
    n = od_w_in.shape[0]
    pad_lo = jnp.zeros((n, D_MODEL, MLA_NOPE), F32)
    pad_hi = jnp.zeros((n, D_MODEL, LANES - MLA_NOPE - MLA_ROPE), F32)
    w_in = jnp.concatenate([od_w_in[:, :, :Q_LORA + KV_LORA], pad_lo, od_w_in[:, :, Q_LORA + KV_LORA:], pad_hi], axis=-1)
    w_uq = od_w_uq.reshape(n, Q_LORA, MLA_HEADS, MLA_NOPE + MLA_ROPE)
    w_uq = jnp.pad(w_uq, ((0, 0), (0, 0), (0, 0), (0, LANES - MLA_NOPE - MLA_ROPE))).reshape(n, Q_LORA, MLA_HEADS * LANES)
    w_ukv = od_w_ukv.reshape(n, KV_LORA, MLA_HEADS, MLA_NOPE + MLA_V)
    w_uk = jnp.pad(w_ukv[..., :MLA_NOPE], ((0, 0), (0, 0), (0, 0), (0, LANES - MLA_NOPE))).reshape(n, KV_LORA, MLA_HEADS * LANES)
    w_uv = w_ukv[..., MLA_NOPE:].reshape(n, KV_LORA, MLA_HEADS * MLA_V)
    return w_in.astype(BF16), w_uq.astype(BF16), w_uk.astype(BF16), w_uv.astype(BF16)


def kernel(x, ln_mix_g, ln_mlp_g, w_up, w_down, ln_f_g, ev_w_in, ev_w_out, ev_lambda_q1, ev_lambda_k1, ev_lambda_q2, ev_lambda_k2, ev_subln_g, ev_rpb, od_w_in, od_q_norm_g, od_kv_norm_g, od_w_uq, od_w_ukv, od_w_out):
    B, S, D = x.shape
    assert D == D_MODEL and S % ROW_TILE == 0 and S % Q_TILE == 0 and S % GRID_W == 0
    x = x.astype(F32)
    ln_mix = ln_mix_g.astype(F32)[:, None, :]
    ln_mlp = ln_mlp_g.astype(F32)[:, None, :]
    w_up_b, w_down_b = w_up.astype(BF16), w_down.astype(BF16)
    ev_w_in_b, ev_w_out_b, od_w_out_b = ev_w_in.astype(BF16), ev_w_out.astype(BF16), od_w_out.astype(BF16)
    od_w_in_b, od_w_uq_b, od_w_uk_b, od_w_uv_b = _mla_weight_layouts(od_w_in.astype(F32), od_w_uq.astype(F32), od_w_ukv.astype(F32))
    gq = od_q_norm_g.astype(F32)[:, None, :]
    gkv = od_kv_norm_g.astype(F32)[:, None, :]
    even_tables = _rope_tables(S, ROT_DIM, 0, HEAD_DIM)
    mla_tables = _rope_tables(S, MLA_ROPE, MLA_NOPE, LANES)
    final_g = ln_f_g.astype(F32)[None, :]

    for l in range(DEPTH):
        i = l // 2
        last = final_g if l == DEPTH - 1 else None
        if l % 2 == 0:
            lam_init = 0.8 - 0.6 * math.exp(-0.3 * l)
            proj = _even_proj(x, ln_mix, ev_w_in_b, even_tables, l, i)
            lam_params = jnp.stack([ev_lambda_q1[i], ev_lambda_k1[i], ev_lambda_q2[i], ev_lambda_k2[i]]).astype(F32)
            o_a = _diff_attn(proj, lam_params, ev_subln_g[i].astype(F32)[None, :], lam_init)
            o_b = _natten(proj, _natten_bias(ev_rpb[i], S // GRID_W))
            parts, w_out, idx = [o_a, o_b], ev_w_out_b, i
        else:
            q, k, v = _mla_proj(x, ln_mix, od_w_in_b, gq, gkv, od_w_uq_b, od_w_uk_b, od_w_uv_b, mla_tables, l, i)
            parts, w_out, idx = [_mla_attn(q, k, v)], od_w_out_b, i
        x = _out_mlp(x, parts, w_out, idx, ln_mlp, w_up_b, w_down_b, l, last)
    return x
```

```python
import functools
import math

import jax
import jax.numpy as jnp
import numpy as np
from jax import lax
from jax.experimental import pallas as pl
from jax.experimental.pallas import tpu as pltpu

D_MODEL = 1024
DEPTH = 4
HEAD_DIM = 64
A_HEADS = D_MODEL // 256
A_V_DIM = 2 * HEAD_DIM
B_HEADS = D_MODEL // 128
ROT_DIM = HEAD_DIM // 4
ROPE_THETA = 500000.0
GRID_W = 64
WIN_R = 8
WIN_C = 16
MLA_HEADS = D_MODEL // 64
MLA_NOPE = 64
MLA_ROPE = 32
MLA_V = 64
Q_LORA = D_MODEL // 4
KV_LORA = D_MODEL // 8
D_FF = 4 * D_MODEL
EPS = 1e-5
A_QK = A_HEADS * 2 * HEAD_DIM
A_V = A_HEADS * A_V_DIM
B_W = B_HEADS * HEAD_DIM
EVEN_IN = 2 * A_QK + A_V + 3 * B_W

LANES = 128
EVEN_SLABS = EVEN_IN // LANES
MLA_PAIRS = MLA_HEADS // 2
VMEM_LIMIT_BYTES = 56 * 1024 * 1024

ROW_TILE = 512
FF_CHUNK = 1024
Q_TILE = 256

F32 = jnp.float32
BF16 = jnp.bfloat16
LOG2E = math.log2(math.e)


def _params():
    return pltpu.CompilerParams(vmem_limit_bytes=VMEM_LIMIT_BYTES)


def _rms(x, g):
    return x * lax.rsqrt(jnp.mean(x * x, axis=-1, keepdims=True) + EPS) * g


def _rope(x, c, sa, sb, half):
    return x * c + pltpu.roll(x, half, 1) * sa + pltpu.roll(x, LANES - half, 1) * sb


def _rope_angles(seq, rot_dim):
    pos = jnp.arange(seq, dtype=F32)
    inv = ROPE_THETA ** (-jnp.arange(0, rot_dim, 2, dtype=F32) / rot_dim)
    ang = pos[:, None] * inv[None, :]
    return jnp.cos(ang), jnp.sin(ang)


def _rope_paired(x, c, s):
    return x * c + pltpu.roll(x, LANES - MLA_ROPE, 1) * s


def _mla_rope_tables(seq):
    half = MLA_ROPE // 2
    cos, sin = _rope_angles(seq, MLA_ROPE)
    lo = MLA_NOPE
    c = jnp.zeros((seq, LANES), F32).at[:, :lo].set(1.0).at[:, lo:lo + half].set(cos).at[:, lo + half:lo + MLA_ROPE].set(cos)
    s = jnp.zeros((seq, LANES), F32).at[:, lo:lo + half].set(-sin).at[:, lo + half:lo + MLA_ROPE].set(sin)
    return c, s


def _rope_tables(seq, rot_dim, lane_of_dim0, period):
    half = rot_dim // 2
    cos, sin = _rope_angles(seq, rot_dim)
    c = jnp.ones((seq, period), F32)
    sa = jnp.zeros((seq, period), F32)
    sb = jnp.zeros((seq, period), F32)
    lo = lane_of_dim0
    c = c.at[:, lo:lo + half].set(cos).at[:, lo + half:lo + rot_dim].set(cos)
    sb = sb.at[:, lo:lo + half].set(-sin)
    sa = sa.at[:, lo + half:lo + rot_dim].set(sin)
    rep = LANES // period
    return tuple(jnp.tile(t, (1, rep)) for t in (c, sa, sb))


def _even_proj_kernel(x_ref, g_ref, w_ref, c_ref, sa_ref, sb_ref, o_ref):
    h = _rms(x_ref[0], g_ref[...]).astype(BF16)
    c, sa, sb = c_ref[...], sa_ref[...], sb_ref[...]
    group = 4
    for j0 in range(0, EVEN_SLABS, group):
        p = jnp.dot(h, w_ref[:, j0 * LANES:(j0 + group) * LANES], preferred_element_type=F32)
        for jj in range(group):
            j = j0 + jj
            blk = p[:, jj * LANES:(jj + 1) * LANES]
            if j < 2 * A_HEADS:
                blk = _rope(blk, c, sa, sb, ROT_DIM // 2)
            if j < A_HEADS or 3 * A_HEADS <= j < 4 * A_HEADS:
                blk = blk * (HEAD_DIM ** -0.5 * LOG2E)
            o_ref[0, j] = blk.astype(BF16)


def _even_proj(x, g, w, tables, layer, idx):
    B, S, _ = x.shape
    nt = S // ROW_TILE
    tab_spec = pl.BlockSpec((ROW_TILE, LANES), lambda b, t: (t, 0))
    return pl.pallas_call(
        _even_proj_kernel,
        grid=(B, nt),
        in_specs=[
            pl.BlockSpec((1, ROW_TILE, D_MODEL), lambda b, t: (b, t, 0)),
            pl.BlockSpec((None, 1, D_MODEL), lambda b, t: (layer, 0, 0)),
            pl.BlockSpec((None, D_MODEL, EVEN_IN), lambda b, t: (idx, 0, 0)),
            tab_spec, tab_spec, tab_spec,
        ],
        out_specs=pl.BlockSpec((1, EVEN_SLABS, ROW_TILE, LANES), lambda b, t: (b, 0, t, 0)),
        out_shape=jax.ShapeDtypeStruct((B, EVEN_SLABS, S, LANES), BF16),
        compiler_params=_params(),
        name="even_proj",
    )(x, g, w, *tables)


def _softmax_rows(s):
    return jnp.exp2(s - jnp.max(s, axis=-1, keepdims=True)).astype(BF16)


def _scores(q, k):
    return lax.dot_general(q, k, (((1,), (1,)), ((), ())), preferred_element_type=F32)


def _low_lanes(rows):
    return lax.broadcasted_iota(jnp.int32, (rows, LANES), 1) < HEAD_DIM


def _stage_pair_values(v, vx_ref):
    lo = _low_lanes(v.shape[0])
    ones = jnp.ones_like(v)
    vx_ref[0] = jnp.where(lo, v, ones)
    vx_ref[1] = jnp.where(lo, ones, v)


def _pair_output(res_a, res_b, lo):
    num = jnp.where(lo, res_a, res_b)
    den = pltpu.roll(jnp.where(lo, res_b, res_a), HEAD_DIM, 1)
    return num / den


def _diff_attn_kernel(q_ref, k_ref, v_ref, lp_ref, g_ref, o_ref, *, lam_init, seq):
    lp = lp_ref[...]
    lam = (jnp.exp(jnp.sum(lp[0:1] * lp[1:2], axis=-1, keepdims=True))
           - jnp.exp(jnp.sum(lp[2:3] * lp[3:4], axis=-1, keepdims=True)) + lam_init)
    lo = _low_lanes(Q_TILE)
    gain = g_ref[...] * (1.0 - lam_init)

    def softmax_pv(q):
        s = _scores(q, k_ref[0, 0])
        p = jnp.exp2(s - jnp.max(s, axis=-1, keepdims=True))
        l = jnp.sum(p, axis=-1, keepdims=True)
        return jnp.dot(p.astype(BF16), v_ref[0, 0], preferred_element_type=F32) / l

    def body(i, carry):
        r = pl.multiple_of(i * Q_TILE, Q_TILE)
        q = q_ref[0, 0, pl.ds(r, Q_TILE), :]
        zero = jnp.zeros_like(q)
        o = softmax_pv(jnp.where(lo, q, zero)) - lam * softmax_pv(jnp.where(lo, zero, q))
        o_ref[0, pl.ds(r, Q_TILE), :] = _rms(o, gain).astype(BF16)
        return carry

    lax.fori_loop(0, seq // Q_TILE, body, 0, unroll=True)


def _diff_attn(proj, lam_params, subln_g, lam_init):
    B, _, S, _ = proj.shape
    slab = lambda off: pl.BlockSpec((1, 1, S, LANES), lambda b, h: (b, off + h, 0, 0))
    return pl.pallas_call(
        functools.partial(_diff_attn_kernel, lam_init=lam_init, seq=S),
        grid=(B, A_HEADS),
        in_specs=[
            slab(0), slab(A_HEADS), slab(2 * A_HEADS),
            pl.BlockSpec((4, HEAD_DIM), lambda b, h: (0, 0)),
            pl.BlockSpec((1, A_V_DIM), lambda b, h: (0, 0)),
        ],
        out_specs=pl.BlockSpec((1, S, LANES), lambda b, h: (b, 0, h)),
        out_shape=jax.ShapeDtypeStruct((B, S, A_V), BF16),
        compiler_params=_params(),
        name="diff_attn",
    )(proj, proj, proj, lam_params, subln_g)


NAT_ROWS = 4
NAT_WIN = 12
NEG_BIG = -1e30


def _nat_window_start(g, rows):
    return jnp.clip(NAT_ROWS * g - WIN_R // 2, 0, rows - NAT_WIN)


def _natten_bias(rpb, rows):
    heads = rpb.shape[0]
    groups = rows // NAT_ROWS
    c = np.arange(GRID_W)
    cs = np.clip(c - WIN_C // 2, 0, GRID_W - WIN_C)
    col_mask = (c[None, :] >= cs[:, None]) & (c[None, :] < cs[:, None] + WIN_C)
    col_idx = np.clip(c[None, :] - c[:, None], -(WIN_C - 1), WIN_C - 1) + (WIN_C - 1)
    onehot = (col_idx[None] == np.arange(2 * WIN_C - 1)[:, None, None]).astype(np.float32)
    colbias = jnp.einsum("hrj,jqk->hrqk", rpb.astype(F32), jnp.asarray(onehot), precision=lax.Precision.HIGHEST)
    colbias = jnp.where(jnp.asarray(col_mask)[None, None], colbias * LOG2E, NEG_BIG)
    neg = jnp.full((heads, GRID_W, GRID_W), NEG_BIG, F32)
    variants = []
    for g in (0, 1, groups - 1):
        win0 = int(np.clip(NAT_ROWS * g - WIN_R // 2, 0, rows - NAT_WIN))
        per_row = []
        for a in range(NAT_ROWS):
            r = NAT_ROWS * g + a
            r0 = int(np.clip(r - WIN_R // 2, 0, rows - WIN_R))
            blocks = [colbias[:, kr - r + WIN_R - 1] if r0 <= kr < r0 + WIN_R else neg
                      for kr in range(win0, win0 + NAT_WIN)]
            per_row.append(jnp.stack(blocks, axis=2))
        variants.append(jnp.stack(per_row, axis=1))
    return jnp.stack(variants, axis=1).reshape(heads, 3, NAT_ROWS * GRID_W, NAT_WIN * GRID_W)


def _natten_kernel(q_ref, k_ref, v_ref, b_ref, o_ref, vx_ref, *, rows):
    _stage_pair_values(v_ref[0, 0], vx_ref)
    qt = NAT_ROWS * GRID_W
    win = NAT_WIN * GRID_W
    lo = _low_lanes(qt)

    def body(g, carry):
        win0 = _nat_window_start(g, rows)
        variant = lax.shift_right_logical(NAT_ROWS * g - win0, int(math.log2(NAT_ROWS)))
        qstart = pl.multiple_of(g * qt, qt)
        kstart = pl.multiple_of(win0 * GRID_W, GRID_W)
        q = q_ref[0, 0, pl.ds(qstart, qt), :]
        ks = k_ref[0, 0, pl.ds(kstart, win), :]
        zero = jnp.zeros_like(q)
        res = []
        for hd in range(2):
            qm = jnp.where(lo, q, zero) if hd == 0 else jnp.where(lo, zero, q)
            p = _softmax_rows(_scores(qm, ks) + b_ref[hd, variant])
            res.append(jnp.dot(p, vx_ref[hd, pl.ds(kstart, win), :], preferred_element_type=F32))
        o_ref[0, pl.ds(qstart, qt), :] = _pair_output(res[0], res[1], lo).astype(BF16)
        return carry

    lax.fori_loop(0, rows // NAT_ROWS, body, 0, unroll=True)


def _natten(proj, bias):
    B, _, S, _ = proj.shape
    rows = S // GRID_W
    assert NAT_ROWS == WIN_R // 2 and rows % NAT_ROWS == 0 and rows >= NAT_WIN
    base = 3 * A_HEADS
    pairs = B_HEADS // 2
    slab = lambda off: pl.BlockSpec((1, 1, S, LANES), lambda p, b: (b, off + p, 0, 0))
    return pl.pallas_call(
        functools.partial(_natten_kernel, rows=rows),
        grid=(pairs, B),
        in_specs=[
            slab(base), slab(base + pairs), slab(base + 2 * pairs),
            pl.BlockSpec((2, 3, NAT_ROWS * GRID_W, NAT_WIN * GRID_W), lambda p, b: (p, 0, 0, 0)),
        ],
        out_specs=pl.BlockSpec((1, S, LANES), lambda p, b: (b, 0, p)),
        out_shape=jax.ShapeDtypeStruct((B, S, B_W), BF16),
        scratch_shapes=[pltpu.VMEM((2, S, LANES), BF16)],
        compiler_params=_params(),
        name="natten",
    )(proj, proj, proj, bias)


def _mla_proj_kernel(x_ref, g_ref, win_ref, gq_ref, gkv_ref, wuq_ref, wuk_ref, wuv_ref,
                     c_ref, s_ref, q_ref, k_ref, v_ref):
    h = _rms(x_ref[0], g_ref[...]).astype(BF16)
    c, s = c_ref[...], s_ref[...]
    lat = jnp.dot(h, win_ref[...], preferred_element_type=F32)
    cq = _rms(lat[:, :Q_LORA], gq_ref[...]).astype(BF16)
    ckv = _rms(lat[:, Q_LORA:Q_LORA + KV_LORA], gkv_ref[...]).astype(BF16)
    k_rope = _rope_paired(lat[:, Q_LORA + KV_LORA:], c, s)
    scale = (MLA_NOPE + MLA_ROPE) ** -0.5 * LOG2E
    group = 4
    for j0 in range(0, MLA_HEADS, group):
        cols = slice(j0 * LANES, (j0 + group) * LANES)
        qg = jnp.dot(cq, wuq_ref[:, cols], preferred_element_type=F32)
        kg = jnp.dot(ckv, wuk_ref[:, cols], preferred_element_type=F32)
        for jj in range(group):
            sl = slice(jj * LANES, (jj + 1) * LANES)
            q_ref[0, j0 + jj] = (_rope_paired(qg[:, sl], c, s) * scale).astype(BF16)
            k_ref[0, j0 + jj] = (kg[:, sl] + k_rope).astype(BF16)
    vv = jnp.dot(ckv, wuv_ref[...], preferred_element_type=F32)
    for j in range(MLA_PAIRS):
        v_ref[0, j] = vv[:, j * LANES:(j + 1) * LANES].astype(BF16)


def _mla_proj(x, g, w_in, gq, gkv, w_uq, w_uk, w_uv, tables, layer, idx):
    B, S, _ = x.shape
    nt = S // ROW_TILE
    tab_spec = pl.BlockSpec((ROW_TILE, LANES), lambda b, t: (t, 0))
    whole = lambda a: pl.BlockSpec((None,) + a.shape[1:], lambda b, t: (idx,) + (0,) * (a.ndim - 1))
    head_out = lambda n: pl.BlockSpec((1, n, ROW_TILE, LANES), lambda b, t: (b, 0, t, 0))
    return pl.pallas_call(
        _mla_proj_kernel,
        grid=(B, nt),
        in_specs=[
            pl.BlockSpec((1, ROW_TILE, D_MODEL), lambda b, t: (b, t, 0)),
            pl.BlockSpec((None, 1, D_MODEL), lambda b, t: (layer, 0, 0)),
            whole(w_in), whole(gq), whole(gkv), whole(w_uq), whole(w_uk), whole(w_uv),
            tab_spec, tab_spec,
        ],
        out_specs=[head_out(MLA_HEADS), head_out(MLA_HEADS), head_out(MLA_PAIRS)],
        out_shape=[
            jax.ShapeDtypeStruct((B, MLA_HEADS, S, LANES), BF16),
            jax.ShapeDtypeStruct((B, MLA_HEADS, S, LANES), BF16),
            jax.ShapeDtypeStruct((B, MLA_PAIRS, S, LANES), BF16),
        ],
        compiler_params=_params(),
        name="mla_proj",
    )(x, g, w_in, gq, gkv, w_uq, w_uk, w_uv, *tables)


def _mla_attn_kernel(q_ref, k_ref, v_ref, o_ref, vx_ref, *, seq):
    _stage_pair_values(v_ref[0, 0], vx_ref)
    lo = _low_lanes(Q_TILE)

    def body(i, carry):
        r = pl.multiple_of(i * Q_TILE, Q_TILE)
        res = []
        for hd in range(2):
            p = _softmax_rows(_scores(q_ref[0, hd, pl.ds(r, Q_TILE), :], k_ref[0, hd]))
            res.append(jnp.dot(p, vx_ref[hd], preferred_element_type=F32))
        o_ref[0, pl.ds(r, Q_TILE), :] = _pair_output(res[0], res[1], lo).astype(BF16)
        return carry

    lax.fori_loop(0, seq // Q_TILE, body, 0, unroll=True)


def _mla_attn(q, k, v):
    B, _, S, _ = q.shape
    pair = pl.BlockSpec((1, 2, S, LANES), lambda b, p: (b, p, 0, 0))
    return pl.pallas_call(
        functools.partial(_mla_attn_kernel, seq=S),
        grid=(B, MLA_PAIRS),
        in_specs=[pair, pair, pl.BlockSpec((1, 1, S, LANES), lambda b, p: (b, p, 0, 0))],
        out_specs=pl.BlockSpec((1, S, LANES), lambda b, p: (b, 0, p)),
        out_shape=jax.ShapeDtypeStruct((B, S, MLA_HEADS * MLA_V), BF16),
        scratch_shapes=[pltpu.VMEM((2, S, LANES), BF16)],
        compiler_params=_params(),
        name="mla_attn",
    )(q, k, v)


def _out_mlp_kernel(*refs, n_parts, final):
    x_ref = refs[0]
    part_refs = refs[1:1 + n_parts]
    wo_ref, g_ref, wu_ref, wd_ref = refs[1 + n_parts:5 + n_parts]
    gf_ref = refs[5 + n_parts] if final else None
    o_ref = refs[-1]
    attn = jnp.concatenate([p_ref[0] for p_ref in part_refs], axis=-1)
    x = x_ref[0] + jnp.dot(attn, wo_ref[...], preferred_element_type=F32)
    h = _rms(x, g_ref[...]).astype(BF16)
    acc = x
    for c0 in range(0, D_FF, FF_CHUNK):
        u = jnp.dot(h, wu_ref[:, c0:c0 + FF_CHUNK], preferred_element_type=F32)
        u = jnp.square(jnp.maximum(u, 0.0)).astype(BF16)
        acc = acc + jnp.dot(u, wd_ref[c0:c0 + FF_CHUNK, :], preferred_element_type=F32)
    if final:
        acc = _rms(acc, gf_ref[...])
    o_ref[0] = acc


def _out_mlp(x, parts, w_out, idx, g, w_up, w_down, layer, final_g):
    B, S, _ = x.shape
    nt = S // ROW_TILE
    final = final_g is not None
    row = lambda width: pl.BlockSpec((1, ROW_TILE, width), lambda b, t: (b, t, 0))
    in_specs = [row(D_MODEL)] + [row(p.shape[-1]) for p in parts] + [
        pl.BlockSpec((None, D_MODEL, D_MODEL), lambda b, t: (idx, 0, 0)),
        pl.BlockSpec((None, 1, D_MODEL), lambda b, t: (layer, 0, 0)),
        pl.BlockSpec((None, D_MODEL, D_FF), lambda b, t: (layer, 0, 0)),
        pl.BlockSpec((None, D_FF, D_MODEL), lambda b, t: (layer, 0, 0)),
    ]
    args = [x, *parts, w_out, g, w_up, w_down]
    if final:
        in_specs.append(pl.BlockSpec((1, D_MODEL), lambda b, t: (0, 0)))
        args.append(final_g)
    return pl.pallas_call(
        functools.partial(_out_mlp_kernel, n_parts=len(parts), final=final),
        grid=(B, nt),
        in_specs=in_specs,
        out_specs=row(D_MODEL),
        out_shape=jax.ShapeDtypeStruct((B, S, D_MODEL), F32),
        compiler_params=_params(),
        name="out_mlp",
    )(*args)


def _with_swapped_rope(w):
    half = MLA_ROPE // 2
    return jnp.concatenate([w, w[..., -half:], w[..., -MLA_ROPE:-half]], axis=-1)


def _mla_weight_layouts(od_w_in, od_w_uq, od_w_ukv):
    assert MLA_NOPE + 2 * MLA_ROPE == LANES
    n = od_w_in.shape[0]
    pad_lo = jnp.zeros((n, D_MODEL, MLA_NOPE), F32)
    w_in = jnp.concatenate([od_w_in[:, :, :Q_LORA + KV_LORA], pad_lo, _with_swapped_rope(od_w_in[:, :, Q_LORA + KV_LORA:])], axis=-1)
    w_uq = _with_swapped_rope(od_w_uq.reshape(n, Q_LORA, MLA_HEADS, MLA_NOPE + MLA_ROPE)).reshape(n, Q_LORA, MLA_HEADS * LANES)
    w_ukv = od_w_ukv.reshape(n, KV_LORA, MLA_HEADS, MLA_NOPE + MLA_V)
    w_uk = jnp.pad(w_ukv[..., :MLA_NOPE], ((0, 0), (0, 0), (0, 0), (0, LANES - MLA_NOPE))).reshape(n, KV_LORA, MLA_HEADS * LANES)
    w_uv = w_ukv[..., MLA_NOPE:].reshape(n, KV_LORA, MLA_HEADS * MLA_V)
    return w_in.astype(BF16), w_uq.astype(BF16), w_uk.astype(BF16), w_uv.astype(BF16)


def kernel(x, ln_mix_g, ln_mlp_g, w_up, w_down, ln_f_g, ev_w_in, ev_w_out, ev_lambda_q1, ev_lambda_k1, ev_lambda_q2, ev_lambda_k2, ev_subln_g, ev_rpb, od_w_in, od_q_norm_g, od_kv_norm_g, od_w_uq, od_w_ukv, od_w_out):
    B, S, D = x.shape
    assert D == D_MODEL and S % ROW_TILE == 0 and S % Q_TILE == 0 and S % GRID_W == 0
    x = x.astype(F32)
    ln_mix = ln_mix_g.astype(F32)[:, None, :]
    ln_mlp = ln_mlp_g.astype(F32)[:, None, :]
    w_up_b, w_down_b = w_up.astype(BF16), w_down.astype(BF16)
    ev_w_in_b, ev_w_out_b, od_w_out_b = ev_w_in.astype(BF16), ev_w_out.astype(BF16), od_w_out.astype(BF16)
    od_w_in_b, od_w_uq_b, od_w_uk_b, od_w_uv_b = _mla_weight_layouts(od_w_in.astype(F32), od_w_uq.astype(F32), od_w_ukv.astype(F32))
    gq = od_q_norm_g.astype(F32)[:, None, :]
    gkv = od_kv_norm_g.astype(F32)[:, None, :]
    even_tables = _rope_tables(S, ROT_DIM, 0, HEAD_DIM)
    mla_tables = _mla_rope_tables(S)
    final_g = ln_f_g.astype(F32)[None, :]

    for l in range(DEPTH):
        i = l // 2
        last = final_g if l == DEPTH - 1 else None
        if l % 2 == 0:
            lam_init = 0.8 - 0.6 * math.exp(-0.3 * l)
            proj = _even_proj(x, ln_mix, ev_w_in_b, even_tables, l, i)
            lam_params = jnp.stack([ev_lambda_q1[i], ev_lambda_k1[i], ev_lambda_q2[i], ev_lambda_k2[i]]).astype(F32)
            o_a = _diff_attn(proj, lam_params, ev_subln_g[i].astype(F32)[None, :], lam_init)
            o_b = _natten(proj, _natten_bias(ev_rpb[i], S // GRID_W))
            parts, w_out, idx = [o_a, o_b], ev_w_out_b, i
        else:
            q, k, v = _mla_proj(x, ln_mix, od_w_in_b, gq, gkv, od_w_uq_b, od_w_uk_b, od_w_uv_b, mla_tables, l, i)
            parts, w_out, idx = [_mla_attn(q, k, v)], od_w_out_b, i
        x = _out_mlp(x, parts, w_out, idx, ln_mlp, w_up_b, w_down_b, l, last)
    return x
```

```python
import functools
import math

import jax
import jax.numpy as jnp
import numpy as np
from jax import lax
from jax.experimental import pallas as pl
from jax.experimental.pallas import tpu as pltpu

D_MODEL = 1024
DEPTH = 4
HEAD_DIM = 64
A_HEADS = D_MODEL // 256
A_V_DIM = 2 * HEAD_DIM
B_HEADS = D_MODEL // 128
ROT_DIM = HEAD_DIM // 4
ROPE_THETA = 500000.0
GRID_W = 64
WIN_R = 8
WIN_C = 16
MLA_HEADS = D_MODEL // 64
MLA_NOPE = 64
MLA_ROPE = 32
MLA_V = 64
Q_LORA = D_MODEL // 4
KV_LORA = D_MODEL // 8
D_FF = 4 * D_MODEL
EPS = 1e-5
A_QK = A_HEADS * 2 * HEAD_DIM
A_V = A_HEADS * A_V_DIM
B_W = B_HEADS * HEAD_DIM
EVEN_IN = 2 * A_QK + A_V + 3 * B_W

LANES = 128
EVEN_SLABS = EVEN_IN // LANES
MLA_PAIRS = MLA_HEADS // 2
VMEM_LIMIT_BYTES = 56 * 1024 * 1024

ROW_TILE = 512
FF_CHUNK = 1024
Q_TILE = 256

F32 = jnp.float32
BF16 = jnp.bfloat16
LOG2E = math.log2(math.e)


def _params():
    return pltpu.CompilerParams(vmem_limit_bytes=VMEM_LIMIT_BYTES)


def _rms(x, g):
    return x * lax.rsqrt(jnp.mean(x * x, axis=-1, keepdims=True) + EPS) * g


def _rope(x, c, sa, sb, half):
    return x * c + pltpu.roll(x, half, 1) * sa + pltpu.roll(x, LANES - half, 1) * sb


def _rope_angles(seq, rot_dim):
    pos = jnp.arange(seq, dtype=F32)
    inv = ROPE_THETA ** (-jnp.arange(0, rot_dim, 2, dtype=F32) / rot_dim)
    ang = pos[:, None] * inv[None, :]
    return jnp.cos(ang), jnp.sin(ang)


def _rope_paired(x, c, s):
    return x * c + pltpu.roll(x, LANES - MLA_ROPE, 1) * s


def _mla_rope_tables(seq):
    half = MLA_ROPE // 2
    cos, sin = _rope_angles(seq, MLA_ROPE)
    lo = MLA_NOPE
    c = jnp.zeros((seq, LANES), F32).at[:, :lo].set(1.0).at[:, lo:lo + half].set(cos).at[:, lo + half:lo + MLA_ROPE].set(cos)
    s = jnp.zeros((seq, LANES), F32).at[:, lo:lo + half].set(-sin).at[:, lo + half:lo + MLA_ROPE].set(sin)
    return c, s


def _rope_tables(seq, rot_dim, lane_of_dim0, period):
    half = rot_dim // 2
    cos, sin = _rope_angles(seq, rot_dim)
    c = jnp.ones((seq, period), F32)
    sa = jnp.zeros((seq, period), F32)
    sb = jnp.zeros((seq, period), F32)
    lo = lane_of_dim0
    c = c.at[:, lo:lo + half].set(cos).at[:, lo + half:lo + rot_dim].set(cos)
    sb = sb.at[:, lo:lo + half].set(-sin)
    sa = sa.at[:, lo + half:lo + rot_dim].set(sin)
    rep = LANES // period
    return tuple(jnp.tile(t, (1, rep)) for t in (c, sa, sb))


def _even_proj_kernel(x_ref, g_ref, w_ref, c_ref, sa_ref, sb_ref, o_ref):
    h = _rms(x_ref[0], g_ref[...]).astype(BF16)
    c, sa, sb = c_ref[...], sa_ref[...], sb_ref[...]
    group = 4
    for j0 in range(0, EVEN_SLABS, group):
        p = jnp.dot(h, w_ref[:, j0 * LANES:(j0 + group) * LANES], preferred_element_type=F32)
        for jj in range(group):
            j = j0 + jj
            blk = p[:, jj * LANES:(jj + 1) * LANES]
            if j < 2 * A_HEADS:
                blk = _rope(blk, c, sa, sb, ROT_DIM // 2)
            if j < A_HEADS or 3 * A_HEADS <= j < 4 * A_HEADS:
                blk = blk * (HEAD_DIM ** -0.5 * LOG2E)
            o_ref[0, j] = blk.astype(BF16)


def _even_proj(x, g, w, tables, layer, idx):
    B, S, _ = x.shape
    nt = S // ROW_TILE
    tab_spec = pl.BlockSpec((ROW_TILE, LANES), lambda b, t: (t, 0))
    return pl.pallas_call(
        _even_proj_kernel,
        grid=(B, nt),
        in_specs=[
            pl.BlockSpec((1, ROW_TILE, D_MODEL), lambda b, t: (b, t, 0)),
            pl.BlockSpec((None, 1, D_MODEL), lambda b, t: (layer, 0, 0)),
            pl.BlockSpec((None, D_MODEL, EVEN_IN), lambda b, t: (idx, 0, 0)),
            tab_spec, tab_spec, tab_spec,
        ],
        out_specs=pl.BlockSpec((1, EVEN_SLABS, ROW_TILE, LANES), lambda b, t: (b, 0, t, 0)),
        out_shape=jax.ShapeDtypeStruct((B, EVEN_SLABS, S, LANES), BF16),
        compiler_params=_params(),
        name="even_proj",
    )(x, g, w, *tables)


def _softmax_rows(s):
    return jnp.exp2(s - jnp.max(s, axis=-1, keepdims=True)).astype(BF16)


def _scores(q, k):
    return lax.dot_general(q, k, (((1,), (1,)), ((), ())), preferred_element_type=F32)


def _low_lanes(rows):
    return lax.broadcasted_iota(jnp.int32, (rows, LANES), 1) < HEAD_DIM


def _stage_values(v, vx_ref):
    vx_ref[:, :LANES] = v
    vx_ref[:, LANES:] = jnp.ones_like(v)


def _weighted_values(p, vx):
    res = jnp.dot(p, vx, preferred_element_type=F32)
    return res[:, :LANES] / res[:, LANES:]


def _diff_attn_kernel(q_ref, k_ref, v_ref, lp_ref, g_ref, o_ref, vx_ref, *, lam_init, seq):
    _stage_values(v_ref[0, 0], vx_ref)
    lp = lp_ref[...]
    lam = (jnp.exp(jnp.sum(lp[0:1] * lp[1:2], axis=-1, keepdims=True))
           - jnp.exp(jnp.sum(lp[2:3] * lp[3:4], axis=-1, keepdims=True)) + lam_init)
    lo = _low_lanes(Q_TILE)
    gain = g_ref[...] * (1.0 - lam_init)

    def softmax_pv(q):
        return _weighted_values(_softmax_rows(_scores(q, k_ref[0, 0])), vx_ref[...])

    def body(i, carry):
        r = pl.multiple_of(i * Q_TILE, Q_TILE)
        q = q_ref[0, 0, pl.ds(r, Q_TILE), :]
        zero = jnp.zeros_like(q)
        o = softmax_pv(jnp.where(lo, q, zero)) - lam * softmax_pv(jnp.where(lo, zero, q))
        o_ref[0, pl.ds(r, Q_TILE), :] = _rms(o, gain).astype(BF16)
        return carry

    lax.fori_loop(0, seq // Q_TILE, body, 0, unroll=True)


def _diff_attn(proj, lam_params, subln_g, lam_init):
    B, _, S, _ = proj.shape
    slab = lambda off: pl.BlockSpec((1, 1, S, LANES), lambda b, h: (b, off + h, 0, 0))
    return pl.pallas_call(
        functools.partial(_diff_attn_kernel, lam_init=lam_init, seq=S),
        grid=(B, A_HEADS),
        in_specs=[
            slab(0), slab(A_HEADS), slab(2 * A_HEADS),
            pl.BlockSpec((4, HEAD_DIM), lambda b, h: (0, 0)),
            pl.BlockSpec((1, A_V_DIM), lambda b, h: (0, 0)),
        ],
        out_specs=pl.BlockSpec((1, S, LANES), lambda b, h: (b, 0, h)),
        out_shape=jax.ShapeDtypeStruct((B, S, A_V), BF16),
        scratch_shapes=[pltpu.VMEM((S, 2 * LANES), BF16)],
        compiler_params=_params(),
        name="diff_attn",
    )(proj, proj, proj, lam_params, subln_g)


NAT_ROWS = 4
NAT_WIN = 12
NEG_BIG = -1e30


def _nat_window_start(g, rows):
    return jnp.clip(NAT_ROWS * g - WIN_R // 2, 0, rows - NAT_WIN)


def _natten_bias(rpb, rows):
    heads = rpb.shape[0]
    groups = rows // NAT_ROWS
    c = np.arange(GRID_W)
    cs = np.clip(c - WIN_C // 2, 0, GRID_W - WIN_C)
    col_mask = (c[None, :] >= cs[:, None]) & (c[None, :] < cs[:, None] + WIN_C)
    col_idx = np.clip(c[None, :] - c[:, None], -(WIN_C - 1), WIN_C - 1) + (WIN_C - 1)
    onehot = (col_idx[None] == np.arange(2 * WIN_C - 1)[:, None, None]).astype(np.float32)
    colbias = jnp.einsum("hrj,jqk->hrqk", rpb.astype(F32), jnp.asarray(onehot), precision=lax.Precision.HIGHEST)
    colbias = jnp.where(jnp.asarray(col_mask)[None, None], colbias * LOG2E, NEG_BIG)
    neg = jnp.full((heads, GRID_W, GRID_W), NEG_BIG, F32)
    variants = []
    for g in (0, 1, groups - 1):
        win0 = int(np.clip(NAT_ROWS * g - WIN_R // 2, 0, rows - NAT_WIN))
        per_row = []
        for a in range(NAT_ROWS):
            r = NAT_ROWS * g + a
            r0 = int(np.clip(r - WIN_R // 2, 0, rows - WIN_R))
            blocks = [colbias[:, kr - r + WIN_R - 1] if r0 <= kr < r0 + WIN_R else neg
                      for kr in range(win0, win0 + NAT_WIN)]
            per_row.append(jnp.stack(blocks, axis=2))
        variants.append(jnp.stack(per_row, axis=1))
    return jnp.stack(variants, axis=1).reshape(heads, 3, NAT_ROWS * GRID_W, NAT_WIN * GRID_W)


def _natten_kernel(q_ref, k_ref, v_ref, b_ref, o_ref, vx_ref, *, rows):
    _stage_values(v_ref[0, 0], vx_ref)
    qt = NAT_ROWS * GRID_W
    win = NAT_WIN * GRID_W
    lo = _low_lanes(qt)

    def body(g, carry):
        win0 = _nat_window_start(g, rows)
        variant = lax.shift_right_logical(NAT_ROWS * g - win0, int(math.log2(NAT_ROWS)))
        qstart = pl.multiple_of(g * qt, qt)
        kstart = pl.multiple_of(win0 * GRID_W, GRID_W)
        q = q_ref[0, 0, pl.ds(qstart, qt), :]
        ks = k_ref[0, 0, pl.ds(kstart, win), :]
        vs = vx_ref[pl.ds(kstart, win), :]
        zero = jnp.zeros_like(q)
        outs = []
        for hd in range(2):
            qm = jnp.where(lo, q, zero) if hd == 0 else jnp.where(lo, zero, q)
            outs.append(_weighted_values(_softmax_rows(_scores(qm, ks) + b_ref[hd, variant]), vs))
        o_ref[0, pl.ds(qstart, qt), :] = jnp.where(lo, outs[0], outs[1]).astype(BF16)
        return carry

    lax.fori_loop(0, rows // NAT_ROWS, body, 0, unroll=True)


def _natten(proj, bias):
    B, _, S, _ = proj.shape
    rows = S // GRID_W
    assert NAT_ROWS == WIN_R // 2 and rows % NAT_ROWS == 0 and rows >= NAT_WIN
    base = 3 * A_HEADS
    pairs = B_HEADS // 2
    slab = lambda off: pl.BlockSpec((1, 1, S, LANES), lambda p, b: (b, off + p, 0, 0))
    return pl.pallas_call(
        functools.partial(_natten_kernel, rows=rows),
        grid=(pairs, B),
        in_specs=[
            slab(base), slab(base + pairs), slab(base + 2 * pairs),
            pl.BlockSpec((2, 3, NAT_ROWS * GRID_W, NAT_WIN * GRID_W), lambda p, b: (p, 0, 0, 0)),
        ],
        out_specs=pl.BlockSpec((1, S, LANES), lambda p, b: (b, 0, p)),
        out_shape=jax.ShapeDtypeStruct((B, S, B_W), BF16),
        scratch_shapes=[pltpu.VMEM((S, 2 * LANES), BF16)],
        compiler_params=_params(),
        name="natten",
    )(proj, proj, proj, bias)


def _mla_proj_kernel(x_ref, g_ref, win_ref, gq_ref, gkv_ref, wuq_ref, wuk_ref, wuv_ref,
                     c_ref, s_ref, q_ref, k_ref, v_ref):
    h = _rms(x_ref[0], g_ref[...]).astype(BF16)
    c, s = c_ref[...], s_ref[...]
    lat = jnp.dot(h, win_ref[...], preferred_element_type=F32)
    cq = _rms(lat[:, :Q_LORA], gq_ref[...]).astype(BF16)
    ckv = _rms(lat[:, Q_LORA:Q_LORA + KV_LORA], gkv_ref[...]).astype(BF16)
    k_rope = _rope_paired(lat[:, Q_LORA + KV_LORA:], c, s)
    scale = (MLA_NOPE + MLA_ROPE) ** -0.5 * LOG2E
    group = 4
    for j0 in range(0, MLA_HEADS, group):
        cols = slice(j0 * LANES, (j0 + group) * LANES)
        qg = jnp.dot(cq, wuq_ref[:, cols], preferred_element_type=F32)
        kg = jnp.dot(ckv, wuk_ref[:, cols], preferred_element_type=F32)
        for jj in range(group):
            sl = slice(jj * LANES, (jj + 1) * LANES)
            q_ref[0, j0 + jj] = (_rope_paired(qg[:, sl], c, s) * scale).astype(BF16)
            k_ref[0, j0 + jj] = (kg[:, sl] + k_rope).astype(BF16)
    vv = jnp.dot(ckv, wuv_ref[...], preferred_element_type=F32)
    for j in range(MLA_PAIRS):
        v_ref[0, j] = vv[:, j * LANES:(j + 1) * LANES].astype(BF16)


def _mla_proj(x, g, w_in, gq, gkv, w_uq, w_uk, w_uv, tables, layer, idx):
    B, S, _ = x.shape
    nt = S // ROW_TILE
    tab_spec = pl.BlockSpec((ROW_TILE, LANES), lambda b, t: (t, 0))
    whole = lambda a: pl.BlockSpec((None,) + a.shape[1:], lambda b, t: (idx,) + (0,) * (a.ndim - 1))
    head_out = lambda n: pl.BlockSpec((1, n, ROW_TILE, LANES), lambda b, t: (b, 0, t, 0))
    return pl.pallas_call(
        _mla_proj_kernel,
        grid=(B, nt),
        in_specs=[
            pl.BlockSpec((1, ROW_TILE, D_MODEL), lambda b, t: (b, t, 0)),
            pl.BlockSpec((None, 1, D_MODEL), lambda b, t: (layer, 0, 0)),
            whole(w_in), whole(gq), whole(gkv), whole(w_uq), whole(w_uk), whole(w_uv),
            tab_spec, tab_spec,
        ],
        out_specs=[head_out(MLA_HEADS), head_out(MLA_HEADS), head_out(MLA_PAIRS)],
        out_shape=[
            jax.ShapeDtypeStruct((B, MLA_HEADS, S, LANES), BF16),
            jax.ShapeDtypeStruct((B, MLA_HEADS, S, LANES), BF16),
            jax.ShapeDtypeStruct((B, MLA_PAIRS, S, LANES), BF16),
        ],
        compiler_params=_params(),
        name="mla_proj",
    )(x, g, w_in, gq, gkv, w_uq, w_uk, w_uv, *tables)


def _mla_attn_kernel(q_ref, k_ref, v_ref, o_ref, vx_ref, *, seq):
    _stage_values(v_ref[0, 0], vx_ref)
    lo = _low_lanes(Q_TILE)

    def body(i, carry):
        r = pl.multiple_of(i * Q_TILE, Q_TILE)
        outs = [_weighted_values(_softmax_rows(_scores(q_ref[0, hd, pl.ds(r, Q_TILE), :], k_ref[0, hd])), vx_ref[...])
                for hd in range(2)]
        o_ref[0, pl.ds(r, Q_TILE), :] = jnp.where(lo, outs[0], outs[1]).astype(BF16)
        return carry

    lax.fori_loop(0, seq // Q_TILE, body, 0, unroll=True)


def _mla_attn(q, k, v):
    B, _, S, _ = q.shape
    pair = pl.BlockSpec((1, 2, S, LANES), lambda b, p: (b, p, 0, 0))
    return pl.pallas_call(
        functools.partial(_mla_attn_kernel, seq=S),
        grid=(B, MLA_PAIRS),
        in_specs=[pair, pair, pl.BlockSpec((1, 1, S, LANES), lambda b, p: (b, p, 0, 0))],
        out_specs=pl.BlockSpec((1, S, LANES), lambda b, p: (b, 0, p)),
        out_shape=jax.ShapeDtypeStruct((B, S, MLA_HEADS * MLA_V), BF16),
        scratch_shapes=[pltpu.VMEM((S, 2 * LANES), BF16)],
        compiler_params=_params(),
        name="mla_attn",
    )(q, k, v)


def _out_mlp_kernel(*refs, n_parts, final):
    x_ref = refs[0]
    part_refs = refs[1:1 + n_parts]
    wo_ref, g_ref, wu_ref, wd_ref = refs[1 + n_parts:5 + n_parts]
    gf_ref = refs[5 + n_parts] if final else None
    o_ref = refs[-1]
    attn = jnp.concatenate([p_ref[0] for p_ref in part_refs], axis=-1)
    x = x_ref[0] + jnp.dot(attn, wo_ref[...], preferred_element_type=F32)
    h = _rms(x, g_ref[...]).astype(BF16)
    acc = x
    for c0 in range(0, D_FF, FF_CHUNK):
        u = jnp.dot(h, wu_ref[:, c0:c0 + FF_CHUNK], preferred_element_type=F32)
        u = jnp.square(jnp.maximum(u, 0.0)).astype(BF16)
        acc = acc + jnp.dot(u, wd_ref[c0:c0 + FF_CHUNK, :], preferred_element_type=F32)
    if final:
        acc = _rms(acc, gf_ref[...])
    o_ref[0] = acc


def _out_mlp(x, parts, w_out, idx, g, w_up, w_down, layer, final_g):
    B, S, _ = x.shape
    nt = S // ROW_TILE
    final = final_g is not None
    row = lambda width: pl.BlockSpec((1, ROW_TILE, width), lambda b, t: (b, t, 0))
    in_specs = [row(D_MODEL)] + [row(p.shape[-1]) for p in parts] + [
        pl.BlockSpec((None, D_MODEL, D_MODEL), lambda b, t: (idx, 0, 0)),
        pl.BlockSpec((None, 1, D_MODEL), lambda b, t: (layer, 0, 0)),
        pl.BlockSpec((None, D_MODEL, D_FF), lambda b, t: (layer, 0, 0)),
        pl.BlockSpec((None, D_FF, D_MODEL), lambda b, t: (layer, 0, 0)),
    ]
    args = [x, *parts, w_out, g, w_up, w_down]
    if final:
        in_specs.append(pl.BlockSpec((1, D_MODEL), lambda b, t: (0, 0)))
        args.append(final_g)
    return pl.pallas_call(
        functools.partial(_out_mlp_kernel, n_parts=len(parts), final=final),
        grid=(B, nt),
        in_specs=in_specs,
        out_specs=row(D_MODEL),
        out_shape=jax.ShapeDtypeStruct((B, S, D_MODEL), F32),
        compiler_params=_params(),
        name="out_mlp",
    )(*args)


def _with_swapped_rope(w):
    half = MLA_ROPE // 2
    return jnp.concatenate([w, w[..., -half:], w[..., -MLA_ROPE:-half]], axis=-1)


def _mla_weight_layouts(od_w_in, od_w_uq, od_w_ukv):
    assert MLA_NOPE + 2 * MLA_ROPE == LANES
    n = od_w_in.shape[0]
    pad_lo = jnp.zeros((n, D_MODEL, MLA_NOPE), F32)
    w_in = jnp.concatenate([od_w_in[:, :, :Q_LORA + KV_LORA], pad_lo, _with_swapped_rope(od_w_in[:, :, Q_LORA + KV_LORA:])], axis=-1)
    w_uq = _with_swapped_rope(od_w_uq.reshape(n, Q_LORA, MLA_HEADS, MLA_NOPE + MLA_ROPE)).reshape(n, Q_LORA, MLA_HEADS * LANES)
    w_ukv = od_w_ukv.reshape(n, KV_LORA, MLA_HEADS, MLA_NOPE + MLA_V)
    w_uk = jnp.pad(w_ukv[..., :MLA_NOPE], ((0, 0), (0, 0), (0, 0), (0, LANES - MLA_NOPE))).reshape(n, KV_LORA, MLA_HEADS * LANES)
    w_uv = w_ukv[..., MLA_NOPE:].reshape(n, KV_LORA, MLA_HEADS * MLA_V)
    return w_in.astype(BF16), w_uq.astype(BF16), w_uk.astype(BF16), w_uv.astype(BF16)


def kernel(x, ln_mix_g, ln_mlp_g, w_up, w_down, ln_f_g, ev_w_in, ev_w_out, ev_lambda_q1, ev_lambda_k1, ev_lambda_q2, ev_lambda_k2, ev_subln_g, ev_rpb, od_w_in, od_q_norm_g, od_kv_norm_g, od_w_uq, od_w_ukv, od_w_out):
    B, S, D = x.shape
    assert D == D_MODEL and S % ROW_TILE == 0 and S % Q_TILE == 0 and S % GRID_W == 0
    x = x.astype(F32)
    ln_mix = ln_mix_g.astype(F32)[:, None, :]
    ln_mlp = ln_mlp_g.astype(F32)[:, None, :]
    w_up_b, w_down_b = w_up.astype(BF16), w_down.astype(BF16)
    ev_w_in_b, ev_w_out_b, od_w_out_b = ev_w_in.astype(BF16), ev_w_out.astype(BF16), od_w_out.astype(BF16)
    od_w_in_b, od_w_uq_b, od_w_uk_b, od_w_uv_b = _mla_weight_layouts(od_w_in.astype(F32), od_w_uq.astype(F32), od_w_ukv.astype(F32))
    gq = od_q_norm_g.astype(F32)[:, None, :]
    gkv = od_kv_norm_g.astype(F32)[:, None, :]
    even_tables = _rope_tables(S, ROT_DIM, 0, HEAD_DIM)
    mla_tables = _mla_rope_tables(S)
    final_g = ln_f_g.astype(F32)[None, :]

    for l in range(DEPTH):
        i = l // 2
        last = final_g if l == DEPTH - 1 else None
        if l % 2 == 0:
            lam_init = 0.8 - 0.6 * math.exp(-0.3 * l)
            proj = _even_proj(x, ln_mix, ev_w_in_b, even_tables, l, i)
            lam_params = jnp.stack([ev_lambda_q1[i], ev_lambda_k1[i], ev_lambda_q2[i], ev_lambda_k2[i]]).astype(F32)
            o_a = _diff_attn(proj, lam_params, ev_subln_g[i].astype(F32)[None, :], lam_init)
            o_b = _natten(proj, _natten_bias(ev_rpb[i], S // GRID_W))
            parts, w_out, idx = [o_a, o_b], ev_w_out_b, i
        else:
            q, k, v = _mla_proj(x, ln_mix, od_w_in_b, gq, gkv, od_w_uq_b, od_w_uk_b, od_w_uv_b, mla_tables, l, i)
            parts, w_out, idx = [_mla_attn(q, k, v)], od_w_out_b, i
        x = _out_mlp(x, parts, w_out, idx, ln_mlp, w_up_b, w_down_b, l, last)
    return x
```

```python
import functools
import math

import jax
import jax.numpy as jnp
import numpy as np
from jax import lax
from jax.experimental import pallas as pl
from jax.experimental.pallas import tpu as pltpu

D_MODEL = 1024
DEPTH = 4
HEAD_DIM = 64
A_HEADS = D_MODEL // 256
A_V_DIM = 2 * HEAD_DIM
B_HEADS = D_MODEL // 128
ROT_DIM = HEAD_DIM // 4
ROPE_THETA = 500000.0
GRID_W = 64
WIN_R = 8
WIN_C = 16
MLA_HEADS = D_MODEL // 64
MLA_NOPE = 64
MLA_ROPE = 32
MLA_V = 64
Q_LORA = D_MODEL // 4
KV_LORA = D_MODEL // 8
D_FF = 4 * D_MODEL
EPS = 1e-5
A_QK = A_HEADS * 2 * HEAD_DIM
A_V = A_HEADS * A_V_DIM
B_W = B_HEADS * HEAD_DIM
EVEN_IN = 2 * A_QK + A_V + 3 * B_W

LANES = 128
EVEN_SLABS = EVEN_IN // LANES
MLA_PAIRS = MLA_HEADS // 2
VMEM_LIMIT_BYTES = 56 * 1024 * 1024

ROW_TILE = 512
FF_CHUNK = 1024
Q_TILE = 128

F32 = jnp.float32
BF16 = jnp.bfloat16
LOG2E = math.log2(math.e)


def _params():
    return pltpu.CompilerParams(vmem_limit_bytes=VMEM_LIMIT_BYTES)


def _rms(x, g):
    return x * lax.rsqrt(jnp.mean(x * x, axis=-1, keepdims=True) + EPS) * g


def _rope(x, c, sa, sb, half):
    return x * c + pltpu.roll(x, half, 1) * sa + pltpu.roll(x, LANES - half, 1) * sb


def _rope_angles(seq, rot_dim):
    pos = jnp.arange(seq, dtype=F32)
    inv = ROPE_THETA ** (-jnp.arange(0, rot_dim, 2, dtype=F32) / rot_dim)
    ang = pos[:, None] * inv[None, :]
    return jnp.cos(ang), jnp.sin(ang)


def _rope_paired(x, c, s):
    return x * c + pltpu.roll(x, LANES - MLA_ROPE, 1) * s


def _mla_rope_tables(seq):
    half = MLA_ROPE // 2
    cos, sin = _rope_angles(seq, MLA_ROPE)
    lo = MLA_NOPE
    c = jnp.zeros((seq, LANES), F32).at[:, :lo].set(1.0).at[:, lo:lo + half].set(cos).at[:, lo + half:lo + MLA_ROPE].set(cos)
    s = jnp.zeros((seq, LANES), F32).at[:, lo:lo + half].set(-sin).at[:, lo + half:lo + MLA_ROPE].set(sin)
    return c, s


def _rope_tables(seq, rot_dim, lane_of_dim0, period):
    half = rot_dim // 2
    cos, sin = _rope_angles(seq, rot_dim)
    c = jnp.ones((seq, period), F32)
    sa = jnp.zeros((seq, period), F32)
    sb = jnp.zeros((seq, period), F32)
    lo = lane_of_dim0
    c = c.at[:, lo:lo + half].set(cos).at[:, lo + half:lo + rot_dim].set(cos)
    sb = sb.at[:, lo:lo + half].set(-sin)
    sa = sa.at[:, lo + half:lo + rot_dim].set(sin)
    rep = LANES // period
    return tuple(jnp.tile(t, (1, rep)) for t in (c, sa, sb))


def _even_proj_kernel(x_ref, g_ref, w_ref, c_ref, sa_ref, sb_ref, o_ref):
    h = _rms(x_ref[0], g_ref[...]).astype(BF16)
    c, sa, sb = c_ref[...], sa_ref[...], sb_ref[...]
    group = 4
    for j0 in range(0, EVEN_SLABS, group):
        p = jnp.dot(h, w_ref[:, j0 * LANES:(j0 + group) * LANES], preferred_element_type=F32)
        for jj in range(group):
            j = j0 + jj
            blk = p[:, jj * LANES:(jj + 1) * LANES]
            if j < 2 * A_HEADS:
                blk = _rope(blk, c, sa, sb, ROT_DIM // 2)
            if j < A_HEADS or 3 * A_HEADS <= j < 4 * A_HEADS:
                blk = blk * (HEAD_DIM ** -0.5 * LOG2E)
            o_ref[0, j] = blk.astype(BF16)


def _even_proj(x, g, w, tables, layer, idx):
    B, S, _ = x.shape
    nt = S // ROW_TILE
    tab_spec = pl.BlockSpec((ROW_TILE, LANES), lambda b, t: (t, 0))
    return pl.pallas_call(
        _even_proj_kernel,
        grid=(B, nt),
        in_specs=[
            pl.BlockSpec((1, ROW_TILE, D_MODEL), lambda b, t: (b, t, 0)),
            pl.BlockSpec((None, 1, D_MODEL), lambda b, t: (layer, 0, 0)),
            pl.BlockSpec((None, D_MODEL, EVEN_IN), lambda b, t: (idx, 0, 0)),
            tab_spec, tab_spec, tab_spec,
        ],
        out_specs=pl.BlockSpec((1, EVEN_SLABS, ROW_TILE, LANES), lambda b, t: (b, 0, t, 0)),
        out_shape=jax.ShapeDtypeStruct((B, EVEN_SLABS, S, LANES), BF16),
        compiler_params=_params(),
        name="even_proj",
    )(x, g, w, *tables)


def _softmax_rows(s):
    return jnp.exp2(s - jnp.max(s, axis=-1, keepdims=True)).astype(BF16)


def _scores(q, k):
    return lax.dot_general(q, k, (((1,), (1,)), ((), ())), preferred_element_type=F32)


def _low_lanes(rows):
    return lax.broadcasted_iota(jnp.int32, (rows, LANES), 1) < HEAD_DIM


def _stage_values(v, vx_ref):
    vx_ref[:, :LANES] = v
    vx_ref[:, LANES:] = jnp.ones_like(v)


def _weighted_values(p, vx):
    res = jnp.dot(p, vx, preferred_element_type=F32)
    return res[:, :LANES] / res[:, LANES:]


def _diff_attn_kernel(q_ref, k_ref, v_ref, lp_ref, g_ref, o_ref, vx_ref, *, lam_init, seq):
    _stage_values(v_ref[0, 0], vx_ref)
    lp = lp_ref[...]
    lam = (jnp.exp(jnp.sum(lp[0:1] * lp[1:2], axis=-1, keepdims=True))
           - jnp.exp(jnp.sum(lp[2:3] * lp[3:4], axis=-1, keepdims=True)) + lam_init)
    lo = _low_lanes(Q_TILE)
    gain = g_ref[...] * (1.0 - lam_init)

    def softmax_pv(q):
        return _weighted_values(_softmax_rows(_scores(q, k_ref[0, 0])), vx_ref[...])

    def body(i, carry):
        r = pl.multiple_of(i * Q_TILE, Q_TILE)
        q = q_ref[0, 0, pl.ds(r, Q_TILE), :]
        zero = jnp.zeros_like(q)
        o = softmax_pv(jnp.where(lo, q, zero)) - lam * softmax_pv(jnp.where(lo, zero, q))
        o_ref[0, pl.ds(r, Q_TILE), :] = _rms(o, gain).astype(BF16)
        return carry

    lax.fori_loop(0, seq // Q_TILE, body, 0, unroll=True)


def _diff_attn(proj, lam_params, subln_g, lam_init):
    B, _, S, _ = proj.shape
    slab = lambda off: pl.BlockSpec((1, 1, S, LANES), lambda b, h: (b, off + h, 0, 0))
    return pl.pallas_call(
        functools.partial(_diff_attn_kernel, lam_init=lam_init, seq=S),
        grid=(B, A_HEADS),
        in_specs=[
            slab(0), slab(A_HEADS), slab(2 * A_HEADS),
            pl.BlockSpec((4, HEAD_DIM), lambda b, h: (0, 0)),
            pl.BlockSpec((1, A_V_DIM), lambda b, h: (0, 0)),
        ],
        out_specs=pl.BlockSpec((1, S, LANES), lambda b, h: (b, 0, h)),
        out_shape=jax.ShapeDtypeStruct((B, S, A_V), BF16),
        scratch_shapes=[pltpu.VMEM((S, 2 * LANES), BF16)],
        compiler_params=_params(),
        name="diff_attn",
    )(proj, proj, proj, lam_params, subln_g)


NAT_ROWS = 4
NAT_WIN = 12
NEG_BIG = -1e30


def _nat_window_start(g, rows):
    return jnp.clip(NAT_ROWS * g - WIN_R // 2, 0, rows - NAT_WIN)


def _natten_bias(rpb, rows):
    heads = rpb.shape[0]
    groups = rows // NAT_ROWS
    c = np.arange(GRID_W)
    cs = np.clip(c - WIN_C // 2, 0, GRID_W - WIN_C)
    col_mask = (c[None, :] >= cs[:, None]) & (c[None, :] < cs[:, None] + WIN_C)
    col_idx = np.clip(c[None, :] - c[:, None], -(WIN_C - 1), WIN_C - 1) + (WIN_C - 1)
    onehot = (col_idx[None] == np.arange(2 * WIN_C - 1)[:, None, None]).astype(np.float32)
    colbias = jnp.einsum("hrj,jqk->hrqk", rpb.astype(F32), jnp.asarray(onehot), precision=lax.Precision.HIGHEST)
    colbias = jnp.where(jnp.asarray(col_mask)[None, None], colbias * LOG2E, NEG_BIG)
    slabs = []
    for g in (0, 1, groups - 1):
        win0 = int(np.clip(NAT_ROWS * g - WIN_R // 2, 0, rows - NAT_WIN))
        for a in range(NAT_ROWS):
            r = NAT_ROWS * g + a
            r0 = int(np.clip(r - WIN_R // 2, 0, rows - WIN_R))
            first = r0 - r + WIN_R - 1
            before = r0 - win0
            slabs.append(jnp.pad(colbias[:, first:first + WIN_R], ((0, 0), (before, NAT_WIN - WIN_R - before), (0, 0), (0, 0)),
                                 constant_values=NEG_BIG))
    table = jnp.stack(slabs, axis=1).reshape(heads, 3, NAT_ROWS, NAT_WIN, GRID_W, GRID_W)
    return table.transpose(0, 1, 2, 4, 3, 5).reshape(heads, 3, NAT_ROWS * GRID_W, NAT_WIN * GRID_W)


def _natten_kernel(q_ref, k_ref, v_ref, b_ref, o_ref, vx_ref, *, rows):
    _stage_values(v_ref[0, 0], vx_ref)
    qt = NAT_ROWS * GRID_W
    win = NAT_WIN * GRID_W
    lo = _low_lanes(qt)

    def body(g, carry):
        win0 = _nat_window_start(g, rows)
        variant = lax.shift_right_logical(NAT_ROWS * g - win0, int(math.log2(NAT_ROWS)))
        qstart = pl.multiple_of(g * qt, qt)
        kstart = pl.multiple_of(win0 * GRID_W, GRID_W)
        q = q_ref[0, 0, pl.ds(qstart, qt), :]
        ks = k_ref[0, 0, pl.ds(kstart, win), :]
        vs = vx_ref[pl.ds(kstart, win), :]
        zero = jnp.zeros_like(q)
        outs = []
        for hd in range(2):
            qm = jnp.where(lo, q, zero) if hd == 0 else jnp.where(lo, zero, q)
            outs.append(_weighted_values(_softmax_rows(_scores(qm, ks) + b_ref[hd, variant]), vs))
        o_ref[0, pl.ds(qstart, qt), :] = jnp.where(lo, outs[0], outs[1]).astype(BF16)
        return carry

    lax.fori_loop(0, rows // NAT_ROWS, body, 0, unroll=True)


def _natten(proj, bias):
    B, _, S, _ = proj.shape
    rows = S // GRID_W
    assert NAT_ROWS == WIN_R // 2 and rows % NAT_ROWS == 0 and rows >= NAT_WIN
    base = 3 * A_HEADS
    pairs = B_HEADS // 2
    slab = lambda off: pl.BlockSpec((1, 1, S, LANES), lambda p, b: (b, off + p, 0, 0))
    return pl.pallas_call(
        functools.partial(_natten_kernel, rows=rows),
        grid=(pairs, B),
        in_specs=[
            slab(base), slab(base + pairs), slab(base + 2 * pairs),
            pl.BlockSpec((2, 3, NAT_ROWS * GRID_W, NAT_WIN * GRID_W), lambda p, b: (p, 0, 0, 0)),
        ],
        out_specs=pl.BlockSpec((1, S, LANES), lambda p, b: (b, 0, p)),
        out_shape=jax.ShapeDtypeStruct((B, S, B_W), BF16),
        scratch_shapes=[pltpu.VMEM((S, 2 * LANES), BF16)],
        compiler_params=_params(),
        name="natten",
    )(proj, proj, proj, bias)


def _mla_proj_kernel(x_ref, g_ref, win_ref, gq_ref, gkv_ref, wuq_ref, wuk_ref, wuv_ref,
                     c_ref, s_ref, q_ref, k_ref, v_ref):
    h = _rms(x_ref[0], g_ref[...]).astype(BF16)
    c, s = c_ref[...], s_ref[...]
    lat = jnp.dot(h, win_ref[...], preferred_element_type=F32)
    cq = _rms(lat[:, :Q_LORA], gq_ref[...]).astype(BF16)
    ckv = _rms(lat[:, Q_LORA:Q_LORA + KV_LORA], gkv_ref[...]).astype(BF16)
    k_rope = _rope_paired(lat[:, Q_LORA + KV_LORA:], c, s)
    scale = (MLA_NOPE + MLA_ROPE) ** -0.5 * LOG2E
    group = 4
    for j0 in range(0, MLA_HEADS, group):
        cols = slice(j0 * LANES, (j0 + group) * LANES)
        qg = jnp.dot(cq, wuq_ref[:, cols], preferred_element_type=F32)
        kg = jnp.dot(ckv, wuk_ref[:, cols], preferred_element_type=F32)
        for jj in range(group):
            sl = slice(jj * LANES, (jj + 1) * LANES)
            q_ref[0, j0 + jj] = (_rope_paired(qg[:, sl], c, s) * scale).astype(BF16)
            k_ref[0, j0 + jj] = (kg[:, sl] + k_rope).astype(BF16)
    vv = jnp.dot(ckv, wuv_ref[...], preferred_element_type=F32)
    for j in range(MLA_PAIRS):
        v_ref[0, j] = vv[:, j * LANES:(j + 1) * LANES].astype(BF16)


def _mla_proj(x, g, w_in, gq, gkv, w_uq, w_uk, w_uv, tables, layer, idx):
    B, S, _ = x.shape
    nt = S // ROW_TILE
    tab_spec = pl.BlockSpec((ROW_TILE, LANES), lambda b, t: (t, 0))
    whole = lambda a: pl.BlockSpec((None,) + a.shape[1:], lambda b, t: (idx,) + (0,) * (a.ndim - 1))
    head_out = lambda n: pl.BlockSpec((1, n, ROW_TILE, LANES), lambda b, t: (b, 0, t, 0))
    return pl.pallas_call(
        _mla_proj_kernel,
        grid=(B, nt),
        in_specs=[
            pl.BlockSpec((1, ROW_TILE, D_MODEL), lambda b, t: (b, t, 0)),
            pl.BlockSpec((None, 1, D_MODEL), lambda b, t: (layer, 0, 0)),
            whole(w_in), whole(gq), whole(gkv), whole(w_uq), whole(w_uk), whole(w_uv),
            tab_spec, tab_spec,
        ],
        out_specs=[head_out(MLA_HEADS), head_out(MLA_HEADS), head_out(MLA_PAIRS)],
        out_shape=[
            jax.ShapeDtypeStruct((B, MLA_HEADS, S, LANES), BF16),
            jax.ShapeDtypeStruct((B, MLA_HEADS, S, LANES), BF16),
            jax.ShapeDtypeStruct((B, MLA_PAIRS, S, LANES), BF16),
        ],
        compiler_params=_params(),
        name="mla_proj",
    )(x, g, w_in, gq, gkv, w_uq, w_uk, w_uv, *tables)


def _mla_attn_kernel(q_ref, k_ref, v_ref, o_ref, vx_ref, *, seq):
    _stage_values(v_ref[0, 0], vx_ref)
    lo = _low_lanes(Q_TILE)

    def body(i, carry):
        r = pl.multiple_of(i * Q_TILE, Q_TILE)
        outs = [_weighted_values(_softmax_rows(_scores(q_ref[0, hd, pl.ds(r, Q_TILE), :], k_ref[0, hd])), vx_ref[...])
                for hd in range(2)]
        o_ref[0, pl.ds(r, Q_TILE), :] = jnp.where(lo, outs[0], outs[1]).astype(BF16)
        return carry

    lax.fori_loop(0, seq // Q_TILE, body, 0, unroll=True)


def _mla_attn(q, k, v):
    B, _, S, _ = q.shape
    pair = pl.BlockSpec((1, 2, S, LANES), lambda b, p: (b, p, 0, 0))
    return pl.pallas_call(
        functools.partial(_mla_attn_kernel, seq=S),
        grid=(B, MLA_PAIRS),
        in_specs=[pair, pair, pl.BlockSpec((1, 1, S, LANES), lambda b, p: (b, p, 0, 0))],
        out_specs=pl.BlockSpec((1, S, LANES), lambda b, p: (b, 0, p)),
        out_shape=jax.ShapeDtypeStruct((B, S, MLA_HEADS * MLA_V), BF16),
        scratch_shapes=[pltpu.VMEM((S, 2 * LANES), BF16)],
        compiler_params=_params(),
        name="mla_attn",
    )(q, k, v)


def _out_mlp_kernel(*refs, n_parts, final):
    x_ref = refs[0]
    part_refs = refs[1:1 + n_parts]
    wo_ref, g_ref, wu_ref, wd_ref = refs[1 + n_parts:5 + n_parts]
    gf_ref = refs[5 + n_parts] if final else None
    o_ref = refs[-1]
    attn = jnp.concatenate([p_ref[0] for p_ref in part_refs], axis=-1)
    x = x_ref[0] + jnp.dot(attn, wo_ref[...], preferred_element_type=F32)
    h = _rms(x, g_ref[...]).astype(BF16)
    acc = x
    for c0 in range(0, D_FF, FF_CHUNK):
        u = jnp.dot(h, wu_ref[:, c0:c0 + FF_CHUNK], preferred_element_type=F32)
        u = jnp.square(jnp.maximum(u, 0.0)).astype(BF16)
        acc = acc + jnp.dot(u, wd_ref[c0:c0 + FF_CHUNK, :], preferred_element_type=F32)
    if final:
        acc = _rms(acc, gf_ref[...])
    o_ref[0] = acc


def _out_mlp(x, parts, w_out, idx, g, w_up, w_down, layer, final_g):
    B, S, _ = x.shape
    nt = S // ROW_TILE
    final = final_g is not None
    row = lambda width: pl.BlockSpec((1, ROW_TILE, width), lambda b, t: (b, t, 0))
    in_specs = [row(D_MODEL)] + [row(p.shape[-1]) for p in parts] + [
        pl.BlockSpec((None, D_MODEL, D_MODEL), lambda b, t: (idx, 0, 0)),
        pl.BlockSpec((None, 1, D_MODEL), lambda b, t: (layer, 0, 0)),
        pl.BlockSpec((None, D_MODEL, D_FF), lambda b, t: (layer, 0, 0)),
        pl.BlockSpec((None, D_FF, D_MODEL), lambda b, t: (layer, 0, 0)),
    ]
    args = [x, *parts, w_out, g, w_up, w_down]
    if final:
        in_specs.append(pl.BlockSpec((1, D_MODEL), lambda b, t: (0, 0)))
        args.append(final_g)
    return pl.pallas_call(
        functools.partial(_out_mlp_kernel, n_parts=len(parts), final=final),
        grid=(B, nt),
        in_specs=in_specs,
        out_specs=row(D_MODEL),
        out_shape=jax.ShapeDtypeStruct((B, S, D_MODEL), F32),
        compiler_params=_params(),
        name="out_mlp",
    )(*args)


def _with_swapped_rope(w):
    half = MLA_ROPE // 2
    return jnp.concatenate([w, w[..., -half:], w[..., -MLA_ROPE:-half]], axis=-1)


def _mla_weight_layouts(od_w_in, od_w_uq, od_w_ukv):
    assert MLA_NOPE + 2 * MLA_ROPE == LANES
    n = od_w_in.shape[0]
    pad_lo = jnp.zeros((n, D_MODEL, MLA_NOPE), F32)
    w_in = jnp.concatenate([od_w_in[:, :, :Q_LORA + KV_LORA], pad_lo, _with_swapped_rope(od_w_in[:, :, Q_LORA + KV_LORA:])], axis=-1)
    w_uq = _with_swapped_rope(od_w_uq.reshape(n, Q_LORA, MLA_HEADS, MLA_NOPE + MLA_ROPE)).reshape(n, Q_LORA, MLA_HEADS * LANES)
    w_ukv = od_w_ukv.reshape(n, KV_LORA, MLA_HEADS, MLA_NOPE + MLA_V)
    w_uk = jnp.pad(w_ukv[..., :MLA_NOPE], ((0, 0), (0, 0), (0, 0), (0, LANES - MLA_NOPE))).reshape(n, KV_LORA, MLA_HEADS * LANES)
    w_uv = w_ukv[..., MLA_NOPE:].reshape(n, KV_LORA, MLA_HEADS * MLA_V)
    return w_in.astype(BF16), w_uq.astype(BF16), w_uk.astype(BF16), w_uv.astype(BF16)


def kernel(x, ln_mix_g, ln_mlp_g, w_up, w_down, ln_f_g, ev_w_in, ev_w_out, ev_lambda_q1, ev_lambda_k1, ev_lambda_q2, ev_lambda_k2, ev_subln_g, ev_rpb, od_w_in, od_q_norm_g, od_kv_norm_g, od_w_uq, od_w_ukv, od_w_out):
    B, S, D = x.shape
    assert D == D_MODEL and S % ROW_TILE == 0 and S % Q_TILE == 0 and S % GRID_W == 0
    x = x.astype(F32)
    ln_mix = ln_mix_g.astype(F32)[:, None, :]
    ln_mlp = ln_mlp_g.astype(F32)[:, None, :]
    w_up_b, w_down_b = w_up.astype(BF16), w_down.astype(BF16)
    ev_w_in_b, ev_w_out_b, od_w_out_b = ev_w_in.astype(BF16), ev_w_out.astype(BF16), od_w_out.astype(BF16)
    od_w_in_b, od_w_uq_b, od_w_uk_b, od_w_uv_b = _mla_weight_layouts(od_w_in.astype(F32), od_w_uq.astype(F32), od_w_ukv.astype(F32))
    gq = od_q_norm_g.astype(F32)[:, None, :]
    gkv = od_kv_norm_g.astype(F32)[:, None, :]
    even_tables = _rope_tables(S, ROT_DIM, 0, HEAD_DIM)
    mla_tables = _mla_rope_tables(S)
    final_g = ln_f_g.astype(F32)[None, :]

    for l in range(DEPTH):
        i = l // 2
        last = final_g if l == DEPTH - 1 else None
        if l % 2 == 0:
            lam_init = 0.8 - 0.6 * math.exp(-0.3 * l)
            proj = _even_proj(x, ln_mix, ev_w_in_b, even_tables, l, i)
            lam_params = jnp.stack([ev_lambda_q1[i], ev_lambda_k1[i], ev_lambda_q2[i], ev_lambda_k2[i]]).astype(F32)
            o_a = _diff_attn(proj, lam_params, ev_subln_g[i].astype(F32)[None, :], lam_init)
            o_b = _natten(proj, _natten_bias(ev_rpb[i], S // GRID_W))
            parts, w_out, idx = [o_a, o_b], ev_w_out_b, i
        else:
            q, k, v = _mla_proj(x, ln_mix, od_w_in_b, gq, gkv, od_w_uq_b, od_w_uk_b, od_w_uv_b, mla_tables, l, i)
            parts, w_out, idx = [_mla_attn(q, k, v)], od_w_out_b, i
        x = _out_mlp(x, parts, w_out, idx, ln_mlp, w_up_b, w_down_b, l, last)
    return x
```

```python
import functools
import math

import jax
import jax.numpy as jnp
import numpy as np
from jax import lax
from jax.experimental import pallas as pl
from jax.experimental.pallas import tpu as pltpu

D_MODEL = 1024
DEPTH = 4
HEAD_DIM = 64
A_HEADS = D_MODEL // 256
A_V_DIM = 2 * HEAD_DIM
B_HEADS = D_MODEL // 128
ROT_DIM = HEAD_DIM // 4
ROPE_THETA = 500000.0
GRID_W = 64
WIN_R = 8
WIN_C = 16
MLA_HEADS = D_MODEL // 64
MLA_NOPE = 64
MLA_ROPE = 32
MLA_V = 64
Q_LORA = D_MODEL // 4
KV_LORA = D_MODEL // 8
D_FF = 4 * D_MODEL
EPS = 1e-5
A_QK = A_HEADS * 2 * HEAD_DIM
A_V = A_HEADS * A_V_DIM
B_W = B_HEADS * HEAD_DIM
EVEN_IN = 2 * A_QK + A_V + 3 * B_W

LANES = 128
EVEN_SLABS = EVEN_IN // LANES
MLA_PAIRS = MLA_HEADS // 2
V7X_VMEM_BYTES = 64 * 1024 * 1024

ROW_TILE = 512
FF_CHUNK = 1024
Q_TILE = 128

F32 = jnp.float32
BF16 = jnp.bfloat16
LOG2E = math.log2(math.e)


def _nbytes(shape, dtype):
    return math.prod(1 if d is None else d for d in shape) * jnp.dtype(dtype).itemsize


def _call(body, *, name, grid, in_specs, out_specs, out_shape, args, temp_bytes, scratch_shapes=()):
    outs = out_shape if isinstance(out_shape, (list, tuple)) else [out_shape]
    out_list = out_specs if isinstance(out_specs, (list, tuple)) else [out_specs]
    windows = sum(_nbytes(spec.block_shape, a.dtype) for spec, a in zip([*in_specs, *out_list], [*args, *outs]))
    limit = 2 * windows + sum(_nbytes(m.shape, m.dtype) for m in scratch_shapes) + temp_bytes
    assert limit <= V7X_VMEM_BYTES, (name, limit)
    return pl.pallas_call(
        body, grid=grid, in_specs=in_specs, out_specs=out_specs, out_shape=out_shape, scratch_shapes=list(scratch_shapes),
        compiler_params=pltpu.CompilerParams(vmem_limit_bytes=limit), name=name,
    )(*args)


def _rms(x, g):
    return x * lax.rsqrt(jnp.mean(x * x, axis=-1, keepdims=True) + EPS) * g


def _rope(x, c, sa, sb, half):
    return x * c + pltpu.roll(x, half, 1) * sa + pltpu.roll(x, LANES - half, 1) * sb


def _rope_angles(seq, rot_dim):
    pos = jnp.arange(seq, dtype=F32)
    inv = ROPE_THETA ** (-jnp.arange(0, rot_dim, 2, dtype=F32) / rot_dim)
    ang = pos[:, None] * inv[None, :]
    return jnp.cos(ang), jnp.sin(ang)


def _rope_paired(x, c, s):
    return x * c + pltpu.roll(x, LANES - MLA_ROPE, 1) * s


def _mla_rope_tables(seq):
    half = MLA_ROPE // 2
    cos, sin = _rope_angles(seq, MLA_ROPE)
    lo = MLA_NOPE
    c = jnp.zeros((seq, LANES), F32).at[:, :lo].set(1.0).at[:, lo:lo + half].set(cos).at[:, lo + half:lo + MLA_ROPE].set(cos)
    s = jnp.zeros((seq, LANES), F32).at[:, lo:lo + half].set(-sin).at[:, lo + half:lo + MLA_ROPE].set(sin)
    return c, s


def _rope_tables(seq, rot_dim, lane_of_dim0, period):
    half = rot_dim // 2
    cos, sin = _rope_angles(seq, rot_dim)
    c = jnp.ones((seq, period), F32)
    sa = jnp.zeros((seq, period), F32)
    sb = jnp.zeros((seq, period), F32)
    lo = lane_of_dim0
    c = c.at[:, lo:lo + half].set(cos).at[:, lo + half:lo + rot_dim].set(cos)
    sb = sb.at[:, lo:lo + half].set(-sin)
    sa = sa.at[:, lo + half:lo + rot_dim].set(sin)
    rep = LANES // period
    return tuple(jnp.tile(t, (1, rep)) for t in (c, sa, sb))


def _even_proj_kernel(x_ref, g_ref, w_ref, c_ref, sa_ref, sb_ref, o_ref):
    h = _rms(x_ref[0], g_ref[...]).astype(BF16)
    c, sa, sb = c_ref[...], sa_ref[...], sb_ref[...]
    group = 4
    for j0 in range(0, EVEN_SLABS, group):
        p = jnp.dot(h, w_ref[:, j0 * LANES:(j0 + group) * LANES], preferred_element_type=F32)
        for jj in range(group):
            j = j0 + jj
            blk = p[:, jj * LANES:(jj + 1) * LANES]
            if j < 2 * A_HEADS:
                blk = _rope(blk, c, sa, sb, ROT_DIM // 2)
            if j < A_HEADS or 3 * A_HEADS <= j < 4 * A_HEADS:
                blk = blk * (HEAD_DIM ** -0.5 * LOG2E)
            o_ref[0, j] = blk.astype(BF16)


def _even_proj(x, g, w, tables, layer, idx):
    B, S, _ = x.shape
    nt = S // ROW_TILE
    tab_spec = pl.BlockSpec((ROW_TILE, LANES), lambda b, t: (t, 0))
    temp_bytes = _nbytes((ROW_TILE, D_MODEL), BF16) + 4 * _nbytes((ROW_TILE, 4 * LANES), F32)
    return _call(
        _even_proj_kernel,
        grid=(B, nt),
        in_specs=[
            pl.BlockSpec((1, ROW_TILE, D_MODEL), lambda b, t: (b, t, 0)),
            pl.BlockSpec((None, 1, D_MODEL), lambda b, t: (layer, 0, 0)),
            pl.BlockSpec((None, D_MODEL, EVEN_IN), lambda b, t: (idx, 0, 0)),
            tab_spec, tab_spec, tab_spec,
        ],
        out_specs=pl.BlockSpec((1, EVEN_SLABS, ROW_TILE, LANES), lambda b, t: (b, 0, t, 0)),
        out_shape=jax.ShapeDtypeStruct((B, EVEN_SLABS, S, LANES), BF16),
        args=(x, g, w, *tables), temp_bytes=temp_bytes, name="even_proj",
    )


def _softmax_rows(s):
    return jnp.exp2(s - jnp.max(s, axis=-1, keepdims=True)).astype(BF16)


def _scores(q, k):
    return lax.dot_general(q, k, (((1,), (1,)), ((), ())), preferred_element_type=F32)


def _score_temp_bytes(rows, keys):
    return 4 * (_nbytes((rows, keys), F32) + _nbytes((rows, keys), BF16))


def _low_lanes(rows):
    return lax.broadcasted_iota(jnp.int32, (rows, LANES), 1) < HEAD_DIM


def _stage_values(v, vx_ref):
    vx_ref[:, :LANES] = v
    vx_ref[:, LANES:] = jnp.ones_like(v)


def _weighted_values(p, vx):
    res = jnp.dot(p, vx, preferred_element_type=F32)
    return res[:, :LANES] / res[:, LANES:]


def _diff_attn_kernel(q_ref, k_ref, v_ref, lp_ref, g_ref, o_ref, vx_ref, *, lam_init, seq):
    _stage_values(v_ref[0, 0], vx_ref)
    lp = lp_ref[...]
    lam = (jnp.exp(jnp.sum(lp[0:1] * lp[1:2], axis=-1, keepdims=True))
           - jnp.exp(jnp.sum(lp[2:3] * lp[3:4], axis=-1, keepdims=True)) + lam_init)
    lo = _low_lanes(Q_TILE)
    gain = g_ref[...] * (1.0 - lam_init)

    def softmax_pv(q):
        return _weighted_values(_softmax_rows(_scores(q, k_ref[0, 0])), vx_ref[...])

    def body(i, carry):
        r = pl.multiple_of(i * Q_TILE, Q_TILE)
        q = q_ref[0, 0, pl.ds(r, Q_TILE), :]
        zero = jnp.zeros_like(q)
        o = softmax_pv(jnp.where(lo, q, zero)) - lam * softmax_pv(jnp.where(lo, zero, q))
        o_ref[0, pl.ds(r, Q_TILE), :] = _rms(o, gain).astype(BF16)
        return carry

    lax.fori_loop(0, seq // Q_TILE, body, 0, unroll=True)


def _diff_attn(proj, lam_params, subln_g, lam_init):
    B, _, S, _ = proj.shape
    slab = lambda off: pl.BlockSpec((1, 1, S, LANES), lambda b, h: (b, off + h, 0, 0))
    return _call(
        functools.partial(_diff_attn_kernel, lam_init=lam_init, seq=S),
        grid=(B, A_HEADS),
        in_specs=[
            slab(0), slab(A_HEADS), slab(2 * A_HEADS),
            pl.BlockSpec((4, HEAD_DIM), lambda b, h: (0, 0)),
            pl.BlockSpec((1, A_V_DIM), lambda b, h: (0, 0)),
        ],
        out_specs=pl.BlockSpec((1, S, LANES), lambda b, h: (b, 0, h)),
        out_shape=jax.ShapeDtypeStruct((B, S, A_V), BF16),
        scratch_shapes=[pltpu.VMEM((S, 2 * LANES), BF16)],
        args=(proj, proj, proj, lam_params, subln_g), temp_bytes=_score_temp_bytes(Q_TILE, S), name="diff_attn",
    )


NAT_ROWS = 4
NAT_WIN = 12
NEG_BIG = -1e30


def _nat_window_start(g, rows):
    return jnp.clip(NAT_ROWS * g - WIN_R // 2, 0, rows - NAT_WIN)


def _natten_bias(rpb, rows):
    heads = rpb.shape[0]
    groups = rows // NAT_ROWS
    c = np.arange(GRID_W)
    cs = np.clip(c - WIN_C // 2, 0, GRID_W - WIN_C)
    col_mask = (c[None, :] >= cs[:, None]) & (c[None, :] < cs[:, None] + WIN_C)
    col_idx = np.clip(c[None, :] - c[:, None], -(WIN_C - 1), WIN_C - 1) + (WIN_C - 1)
    onehot = (col_idx[None] == np.arange(2 * WIN_C - 1)[:, None, None]).astype(np.float32)
    colbias = jnp.einsum("hrj,jqk->hqrk", rpb.astype(F32), jnp.asarray(onehot), precision=lax.Precision.HIGHEST)
    colbias = jnp.where(jnp.asarray(col_mask)[None, :, None], colbias * LOG2E, NEG_BIG)
    slabs = []
    for g in (0, 1, groups - 1):
        win0 = int(np.clip(NAT_ROWS * g - WIN_R // 2, 0, rows - NAT_WIN))
        for a in range(NAT_ROWS):
            r = NAT_ROWS * g + a
            r0 = int(np.clip(r - WIN_R // 2, 0, rows - WIN_R))
            first = r0 - r + WIN_R - 1
            before = r0 - win0
            slabs.append(jnp.pad(colbias[:, :, first:first + WIN_R], ((0, 0), (0, 0), (before, NAT_WIN - WIN_R - before), (0, 0)),
                                 constant_values=NEG_BIG))
    return jnp.stack(slabs, axis=1).reshape(heads, 3, NAT_ROWS * GRID_W, NAT_WIN * GRID_W)


def _natten_kernel(q_ref, k_ref, v_ref, b_ref, o_ref, vx_ref, *, rows):
    _stage_values(v_ref[0, 0], vx_ref)
    qt = NAT_ROWS * GRID_W
    win = NAT_WIN * GRID_W
    lo = _low_lanes(qt)

    def body(g, carry):
        win0 = _nat_window_start(g, rows)
        variant = lax.shift_right_logical(NAT_ROWS * g - win0, int(math.log2(NAT_ROWS)))
        qstart = pl.multiple_of(g * qt, qt)
        kstart = pl.multiple_of(win0 * GRID_W, GRID_W)
        q = q_ref[0, 0, pl.ds(qstart, qt), :]
        ks = k_ref[0, 0, pl.ds(kstart, win), :]
        vs = vx_ref[pl.ds(kstart, win), :]
        zero = jnp.zeros_like(q)
        outs = []
        for hd in range(2):
            qm = jnp.where(lo, q, zero) if hd == 0 else jnp.where(lo, zero, q)
            outs.append(_weighted_values(_softmax_rows(_scores(qm, ks) + b_ref[hd, variant]), vs))
        o_ref[0, pl.ds(qstart, qt), :] = jnp.where(lo, outs[0], outs[1]).astype(BF16)
        return carry

    lax.fori_loop(0, rows // NAT_ROWS, body, 0, unroll=True)


def _natten(proj, bias):
    B, _, S, _ = proj.shape
    rows = S // GRID_W
    assert NAT_ROWS == WIN_R // 2 and rows % NAT_ROWS == 0 and rows >= NAT_WIN
    base = 3 * A_HEADS
    pairs = B_HEADS // 2
    slab = lambda off: pl.BlockSpec((1, 1, S, LANES), lambda p, b: (b, off + p, 0, 0))
    return _call(
        functools.partial(_natten_kernel, rows=rows),
        grid=(pairs, B),
        in_specs=[
            slab(base), slab(base + pairs), slab(base + 2 * pairs),
            pl.BlockSpec((2, 3, NAT_ROWS * GRID_W, NAT_WIN * GRID_W), lambda p, b: (p, 0, 0, 0)),
        ],
        out_specs=pl.BlockSpec((1, S, LANES), lambda p, b: (b, 0, p)),
        out_shape=jax.ShapeDtypeStruct((B, S, B_W), BF16),
        scratch_shapes=[pltpu.VMEM((S, 2 * LANES), BF16)],
        args=(proj, proj, proj, bias), temp_bytes=_score_temp_bytes(NAT_ROWS * GRID_W, NAT_WIN * GRID_W), name="natten",
    )


def _mla_proj_kernel(x_ref, g_ref, win_ref, gq_ref, gkv_ref, wuq_ref, wuk_ref, wuv_ref,
                     c_ref, s_ref, q_ref, k_ref, v_ref):
    h = _rms(x_ref[0], g_ref[...]).astype(BF16)
    c, s = c_ref[...], s_ref[...]
    lat = jnp.dot(h, win_ref[...], preferred_element_type=F32)
    cq = _rms(lat[:, :Q_LORA], gq_ref[...]).astype(BF16)
    ckv = _rms(lat[:, Q_LORA:Q_LORA + KV_LORA], gkv_ref[...]).astype(BF16)
    k_rope = _rope_paired(lat[:, Q_LORA + KV_LORA:], c, s)
    scale = (MLA_NOPE + MLA_ROPE) ** -0.5 * LOG2E
    group = 4
    for j0 in range(0, MLA_HEADS, group):
        cols = slice(j0 * LANES, (j0 + group) * LANES)
        qg = jnp.dot(cq, wuq_ref[:, cols], preferred_element_type=F32)
        kg = jnp.dot(ckv, wuk_ref[:, cols], preferred_element_type=F32)
        for jj in range(group):
            sl = slice(jj * LANES, (jj + 1) * LANES)
            q_ref[0, j0 + jj] = (_rope_paired(qg[:, sl], c, s) * scale).astype(BF16)
            k_ref[0, j0 + jj] = (kg[:, sl] + k_rope).astype(BF16)
    vv = jnp.dot(ckv, wuv_ref[...], preferred_element_type=F32)
    for j in range(MLA_PAIRS):
        v_ref[0, j] = vv[:, j * LANES:(j + 1) * LANES].astype(BF16)


def _mla_proj(x, g, w_in, gq, gkv, w_uq, w_uk, w_uv, tables, layer, idx):
    B, S, _ = x.shape
    nt = S // ROW_TILE
    tab_spec = pl.BlockSpec((ROW_TILE, LANES), lambda b, t: (t, 0))
    whole = lambda a: pl.BlockSpec((None,) + a.shape[1:], lambda b, t: (idx,) + (0,) * (a.ndim - 1))
    head_out = lambda n: pl.BlockSpec((1, n, ROW_TILE, LANES), lambda b, t: (b, 0, t, 0))
    temp_bytes = (_nbytes((ROW_TILE, D_MODEL), BF16) + _nbytes((ROW_TILE, w_in.shape[-1]), F32)
                  + 4 * _nbytes((ROW_TILE, 4 * LANES), F32) + _nbytes((ROW_TILE, w_uv.shape[-1]), F32))
    return _call(
        _mla_proj_kernel,
        grid=(B, nt),
        in_specs=[
            pl.BlockSpec((1, ROW_TILE, D_MODEL), lambda b, t: (b, t, 0)),
            pl.BlockSpec((None, 1, D_MODEL), lambda b, t: (layer, 0, 0)),
            whole(w_in), whole(gq), whole(gkv), whole(w_uq), whole(w_uk), whole(w_uv),
            tab_spec, tab_spec,
        ],
        out_specs=[head_out(MLA_HEADS), head_out(MLA_HEADS), head_out(MLA_PAIRS)],
        out_shape=[
            jax.ShapeDtypeStruct((B, MLA_HEADS, S, LANES), BF16),
            jax.ShapeDtypeStruct((B, MLA_HEADS, S, LANES), BF16),
            jax.ShapeDtypeStruct((B, MLA_PAIRS, S, LANES), BF16),
        ],
        args=(x, g, w_in, gq, gkv, w_uq, w_uk, w_uv, *tables), temp_bytes=temp_bytes, name="mla_proj",
    )


def _mla_attn_kernel(q_ref, k_ref, v_ref, o_ref, vx_ref, *, seq):
    _stage_values(v_ref[0, 0], vx_ref)
    lo = _low_lanes(Q_TILE)

    def body(i, carry):
        r = pl.multiple_of(i * Q_TILE, Q_TILE)
        outs = [_weighted_values(_softmax_rows(_scores(q_ref[0, hd, pl.ds(r, Q_TILE), :], k_ref[0, hd])), vx_ref[...])
                for hd in range(2)]
        o_ref[0, pl.ds(r, Q_TILE), :] = jnp.where(lo, outs[0], outs[1]).astype(BF16)
        return carry

    lax.fori_loop(0, seq // Q_TILE, body, 0, unroll=True)


def _mla_attn(q, k, v):
    B, _, S, _ = q.shape
    pair = pl.BlockSpec((1, 2, S, LANES), lambda b, p: (b, p, 0, 0))
    return _call(
        functools.partial(_mla_attn_kernel, seq=S),
        grid=(B, MLA_PAIRS),
        in_specs=[pair, pair, pl.BlockSpec((1, 1, S, LANES), lambda b, p: (b, p, 0, 0))],
        out_specs=pl.BlockSpec((1, S, LANES), lambda b, p: (b, 0, p)),
        out_shape=jax.ShapeDtypeStruct((B, S, MLA_HEADS * MLA_V), BF16),
        scratch_shapes=[pltpu.VMEM((S, 2 * LANES), BF16)],
        args=(q, k, v), temp_bytes=_score_temp_bytes(Q_TILE, S), name="mla_attn",
    )


def _out_mlp_kernel(*refs, n_parts, final):
    x_ref = refs[0]
    part_refs = refs[1:1 + n_parts]
    wo_ref, g_ref, wu_ref, wd_ref = refs[1 + n_parts:5 + n_parts]
    gf_ref = refs[5 + n_parts] if final else None
    o_ref = refs[-1]
    attn = jnp.concatenate([p_ref[0] for p_ref in part_refs], axis=-1)
    x = x_ref[0] + jnp.dot(attn, wo_ref[...], preferred_element_type=F32)
    h = _rms(x, g_ref[...]).astype(BF16)
    acc = x
    for c0 in range(0, D_FF, FF_CHUNK):
        u = jnp.dot(h, wu_ref[:, c0:c0 + FF_CHUNK], preferred_element_type=F32)
        u = jnp.square(jnp.maximum(u, 0.0)).astype(BF16)
        acc = acc + jnp.dot(u, wd_ref[c0:c0 + FF_CHUNK, :], preferred_element_type=F32)
    if final:
        acc = _rms(acc, gf_ref[...])
    o_ref[0] = acc


def _out_mlp(x, parts, w_out, idx, g, w_up, w_down, layer, final_g):
    B, S, _ = x.shape
    nt = S // ROW_TILE
    final = final_g is not None
    row = lambda width: pl.BlockSpec((1, ROW_TILE, width), lambda b, t: (b, t, 0))
    in_specs = [row(D_MODEL)] + [row(p.shape[-1]) for p in parts] + [
        pl.BlockSpec((None, D_MODEL, D_MODEL), lambda b, t: (idx, 0, 0)),
        pl.BlockSpec((None, 1, D_MODEL), lambda b, t: (layer, 0, 0)),
        pl.BlockSpec((None, D_MODEL, D_FF), lambda b, t: (layer, 0, 0)),
        pl.BlockSpec((None, D_FF, D_MODEL), lambda b, t: (layer, 0, 0)),
    ]
    args = [x, *parts, w_out, g, w_up, w_down]
    if final:
        in_specs.append(pl.BlockSpec((1, D_MODEL), lambda b, t: (0, 0)))
        args.append(final_g)
    temp_bytes = (2 * _nbytes((ROW_TILE, D_MODEL), F32) + _nbytes((ROW_TILE, D_MODEL), BF16)
                  + _nbytes((ROW_TILE, FF_CHUNK), F32) + _nbytes((ROW_TILE, FF_CHUNK), BF16))
    return _call(
        functools.partial(_out_mlp_kernel, n_parts=len(parts), final=final),
        grid=(B, nt),
        in_specs=in_specs,
        out_specs=row(D_MODEL),
        out_shape=jax.ShapeDtypeStruct((B, S, D_MODEL), F32),
        args=args, temp_bytes=temp_bytes, name="out_mlp",
    )


def _with_swapped_rope(w):
    half = MLA_ROPE // 2
    return jnp.concatenate([w, w[..., -half:], w[..., -MLA_ROPE:-half]], axis=-1)


def _mla_weight_layouts(od_w_in, od_w_uq, od_w_ukv):
    assert MLA_NOPE + 2 * MLA_ROPE == LANES
    n = od_w_in.shape[0]
    pad_lo = jnp.zeros((n, D_MODEL, MLA_NOPE), F32)
    w_in = jnp.concatenate([od_w_in[:, :, :Q_LORA + KV_LORA], pad_lo, _with_swapped_rope(od_w_in[:, :, Q_LORA + KV_LORA:])], axis=-1)
    w_uq = _with_swapped_rope(od_w_uq.reshape(n, Q_LORA, MLA_HEADS, MLA_NOPE + MLA_ROPE)).reshape(n, Q_LORA, MLA_HEADS * LANES)
    w_ukv = od_w_ukv.reshape(n, KV_LORA, MLA_HEADS, MLA_NOPE + MLA_V)
    w_uk = jnp.pad(w_ukv[..., :MLA_NOPE], ((0, 0), (0, 0), (0, 0), (0, LANES - MLA_NOPE))).reshape(n, KV_LORA, MLA_HEADS * LANES)
    w_uv = w_ukv[..., MLA_NOPE:].reshape(n, KV_LORA, MLA_HEADS * MLA_V)
    return w_in.astype(BF16), w_uq.astype(BF16), w_uk.astype(BF16), w_uv.astype(BF16)


def kernel(x, ln_mix_g, ln_mlp_g, w_up, w_down, ln_f_g, ev_w_in, ev_w_out, ev_lambda_q1, ev_lambda_k1, ev_lambda_q2, ev_lambda_k2, ev_subln_g, ev_rpb, od_w_in, od_q_norm_g, od_kv_norm_g, od_w_uq, od_w_ukv, od_w_out):
    B, S, D = x.shape
    assert D == D_MODEL and S % ROW_TILE == 0 and S % Q_TILE == 0 and S % GRID_W == 0
    x = x.astype(F32)
    ln_mix = ln_mix_g.astype(F32)[:, None, :]
    ln_mlp = ln_mlp_g.astype(F32)[:, None, :]
    w_up_b, w_down_b = w_up.astype(BF16), w_down.astype(BF16)
    ev_w_in_b, ev_w_out_b, od_w_out_b = ev_w_in.astype(BF16), ev_w_out.astype(BF16), od_w_out.astype(BF16)
    od_w_in_b, od_w_uq_b, od_w_uk_b, od_w_uv_b = _mla_weight_layouts(od_w_in.astype(F32), od_w_uq.astype(F32), od_w_ukv.astype(F32))
    gq = od_q_norm_g.astype(F32)[:, None, :]
    gkv = od_kv_norm_g.astype(F32)[:, None, :]
    even_tables = _rope_tables(S, ROT_DIM, 0, HEAD_DIM)
    mla_tables = _mla_rope_tables(S)
    final_g = ln_f_g.astype(F32)[None, :]

    for l in range(DEPTH):
        i = l // 2
        last = final_g if l == DEPTH - 1 else None
        if l % 2 == 0:
            lam_init = 0.8 - 0.6 * math.exp(-0.3 * l)
            proj = _even_proj(x, ln_mix, ev_w_in_b, even_tables, l, i)
            lam_params = jnp.stack([ev_lambda_q1[i], ev_lambda_k1[i], ev_lambda_q2[i], ev_lambda_k2[i]]).astype(F32)
            o_a = _diff_attn(proj, lam_params, ev_subln_g[i].astype(F32)[None, :], lam_init)
            o_b = _natten(proj, _natten_bias(ev_rpb[i], S // GRID_W))
            parts, w_out, idx = [o_a, o_b], ev_w_out_b, i
        else:
            q, k, v = _mla_proj(x, ln_mix, od_w_in_b, gq, gkv, od_w_uq_b, od_w_uk_b, od_w_uv_b, mla_tables, l, i)
            parts, w_out, idx = [_mla_attn(q, k, v)], od_w_out_b, i
        x = _out_mlp(x, parts, w_out, idx, ln_mlp, w_up_b, w_down_b, l, last)
    return x
```

```python
import functools
import math

import jax
import jax.numpy as jnp
import numpy as np
from jax import lax
from jax.experimental import pallas as pl
from jax.experimental.pallas import tpu as pltpu

D_MODEL = 1024
DEPTH = 4
HEAD_DIM = 64
A_HEADS = D_MODEL // 256
A_V_DIM = 2 * HEAD_DIM
B_HEADS = D_MODEL // 128
ROT_DIM = HEAD_DIM // 4
ROPE_THETA = 500000.0
GRID_W = 64
WIN_R = 8
WIN_C = 16
MLA_HEADS = D_MODEL // 64
MLA_NOPE = 64
MLA_ROPE = 32
MLA_V = 64
Q_LORA = D_MODEL // 4
KV_LORA = D_MODEL // 8
D_FF = 4 * D_MODEL
EPS = 1e-5
A_QK = A_HEADS * 2 * HEAD_DIM
A_V = A_HEADS * A_V_DIM
B_W = B_HEADS * HEAD_DIM
EVEN_IN = 2 * A_QK + A_V + 3 * B_W

LANES = 128
EVEN_SLABS = EVEN_IN // LANES
MLA_PAIRS = MLA_HEADS // 2
V7X_VMEM_BYTES = 64 * 1024 * 1024

ROW_TILE = 512
FF_CHUNK = 1024
Q_TILE = 128

F32 = jnp.float32
BF16 = jnp.bfloat16
LOG2E = math.log2(math.e)


def _nbytes(shape, dtype):
    return math.prod(1 if d is None else d for d in shape) * jnp.dtype(dtype).itemsize


def _call(body, *, name, grid, in_specs, out_specs, out_shape, args, temp_bytes, scratch_shapes=()):
    outs = out_shape if isinstance(out_shape, (list, tuple)) else [out_shape]
    out_list = out_specs if isinstance(out_specs, (list, tuple)) else [out_specs]
    windows = sum(_nbytes(spec.block_shape, a.dtype) for spec, a in zip([*in_specs, *out_list], [*args, *outs]))
    limit = 2 * windows + sum(_nbytes(m.shape, m.dtype) for m in scratch_shapes) + temp_bytes
    assert limit <= V7X_VMEM_BYTES, (name, limit)
    return pl.pallas_call(
        body, grid=grid, in_specs=in_specs, out_specs=out_specs, out_shape=out_shape, scratch_shapes=list(scratch_shapes),
        compiler_params=pltpu.CompilerParams(vmem_limit_bytes=limit), name=name,
    )(*args)


def _rms(x, g):
    return x * lax.rsqrt(jnp.mean(x * x, axis=-1, keepdims=True) + EPS) * g


def _rope(x, c, sa, sb, half):
    return x * c + pltpu.roll(x, half, 1) * sa + pltpu.roll(x, LANES - half, 1) * sb


def _rope_angles(seq, rot_dim):
    pos = jnp.arange(seq, dtype=F32)
    inv = ROPE_THETA ** (-jnp.arange(0, rot_dim, 2, dtype=F32) / rot_dim)
    ang = pos[:, None] * inv[None, :]
    return jnp.cos(ang), jnp.sin(ang)


def _rope_paired(x, c, s):
    return x * c + pltpu.roll(x, LANES - MLA_ROPE, 1) * s


def _mla_rope_tables(seq):
    cos, sin = _rope_angles(seq, MLA_ROPE)
    ones = jnp.ones((seq, MLA_NOPE), F32)
    zeros = jnp.zeros((seq, MLA_ROPE), F32)
    c = jnp.concatenate([ones, cos, cos, zeros], axis=-1)
    s = jnp.concatenate([jnp.zeros_like(ones), -sin, sin, zeros], axis=-1)
    return c, s


def _rope_tables(seq):
    cos, sin = _rope_angles(seq, ROT_DIM)
    zh = jnp.zeros_like(sin)
    rest = HEAD_DIM - ROT_DIM
    c = jnp.concatenate([cos, cos, jnp.ones((seq, rest), F32)], axis=-1)
    sa = jnp.concatenate([zh, sin, jnp.zeros((seq, rest), F32)], axis=-1)
    sb = jnp.concatenate([-sin, zh, jnp.zeros((seq, rest), F32)], axis=-1)
    return tuple(jnp.tile(t, (1, LANES // HEAD_DIM)) for t in (c, sa, sb))


def _even_proj_kernel(x_ref, g_ref, w_ref, c_ref, sa_ref, sb_ref, o_ref):
    h = _rms(x_ref[0], g_ref[...]).astype(BF16)
    c, sa, sb = c_ref[...], sa_ref[...], sb_ref[...]
    group = 4
    for j0 in range(0, EVEN_SLABS, group):
        p = jnp.dot(h, w_ref[:, j0 * LANES:(j0 + group) * LANES], preferred_element_type=F32)
        for jj in range(group):
            j = j0 + jj
            blk = p[:, jj * LANES:(jj + 1) * LANES]
            if j < 2 * A_HEADS:
                blk = _rope(blk, c, sa, sb, ROT_DIM // 2)
            if j < A_HEADS or 3 * A_HEADS <= j < 4 * A_HEADS:
                blk = blk * (HEAD_DIM ** -0.5 * LOG2E)
            o_ref[0, j] = blk.astype(BF16)


def _even_proj(x, g, w, tables, layer, idx):
    B, S, _ = x.shape
    nt = S // ROW_TILE
    tab_spec = pl.BlockSpec((ROW_TILE, LANES), lambda b, t: (t, 0))
    temp_bytes = _nbytes((ROW_TILE, D_MODEL), BF16) + 4 * _nbytes((ROW_TILE, 4 * LANES), F32)
    return _call(
        _even_proj_kernel,
        grid=(B, nt),
        in_specs=[
            pl.BlockSpec((1, ROW_TILE, D_MODEL), lambda b, t: (b, t, 0)),
            pl.BlockSpec((None, 1, D_MODEL), lambda b, t: (layer, 0, 0)),
            pl.BlockSpec((None, D_MODEL, EVEN_IN), lambda b, t: (idx, 0, 0)),
            tab_spec, tab_spec, tab_spec,
        ],
        out_specs=pl.BlockSpec((1, EVEN_SLABS, ROW_TILE, LANES), lambda b, t: (b, 0, t, 0)),
        out_shape=jax.ShapeDtypeStruct((B, EVEN_SLABS, S, LANES), BF16),
        args=(x, g, w, *tables), temp_bytes=temp_bytes, name="even_proj",
    )


def _softmax_rows(s):
    return jnp.exp2(s - jnp.max(s, axis=-1, keepdims=True)).astype(BF16)


def _scores(q, k):
    return lax.dot_general(q, k, (((1,), (1,)), ((), ())), preferred_element_type=F32)


def _score_temp_bytes(rows, keys):
    return 4 * (_nbytes((rows, keys), F32) + _nbytes((rows, keys), BF16))


def _low_lanes(rows):
    return lax.broadcasted_iota(jnp.int32, (rows, LANES), 1) < HEAD_DIM


def _stage_values(v, vx_ref):
    vx_ref[:, :LANES] = v
    vx_ref[:, LANES:] = jnp.ones_like(v)


def _weighted_values(p, vx):
    res = jnp.dot(p, vx, preferred_element_type=F32)
    return res[:, :LANES] / res[:, LANES:]


def _diff_attn_kernel(q_ref, k_ref, v_ref, lp_ref, g_ref, o_ref, vx_ref, *, lam_init, seq):
    _stage_values(v_ref[0, 0], vx_ref)
    lp = lp_ref[...]
    lam = (jnp.exp(jnp.sum(lp[0:1] * lp[1:2], axis=-1, keepdims=True))
           - jnp.exp(jnp.sum(lp[2:3] * lp[3:4], axis=-1, keepdims=True)) + lam_init)
    lo = _low_lanes(Q_TILE)
    gain = g_ref[...] * (1.0 - lam_init)

    def softmax_pv(q):
        return _weighted_values(_softmax_rows(_scores(q, k_ref[0, 0])), vx_ref[...])

    def body(i, carry):
        r = pl.multiple_of(i * Q_TILE, Q_TILE)
        q = q_ref[0, 0, pl.ds(r, Q_TILE), :]
        zero = jnp.zeros_like(q)
        o = softmax_pv(jnp.where(lo, q, zero)) - lam * softmax_pv(jnp.where(lo, zero, q))
        o_ref[0, pl.ds(r, Q_TILE), :] = _rms(o, gain).astype(BF16)
        return carry

    lax.fori_loop(0, seq // Q_TILE, body, 0, unroll=True)


def _diff_attn(proj, lam_params, subln_g, lam_init):
    B, _, S, _ = proj.shape
    slab = lambda off: pl.BlockSpec((1, 1, S, LANES), lambda b, h: (b, off + h, 0, 0))
    return _call(
        functools.partial(_diff_attn_kernel, lam_init=lam_init, seq=S),
        grid=(B, A_HEADS),
        in_specs=[
            slab(0), slab(A_HEADS), slab(2 * A_HEADS),
            pl.BlockSpec((4, HEAD_DIM), lambda b, h: (0, 0)),
            pl.BlockSpec((1, A_V_DIM), lambda b, h: (0, 0)),
        ],
        out_specs=pl.BlockSpec((1, S, LANES), lambda b, h: (b, 0, h)),
        out_shape=jax.ShapeDtypeStruct((B, S, A_V), BF16),
        scratch_shapes=[pltpu.VMEM((S, 2 * LANES), BF16)],
        args=(proj, proj, proj, lam_params, subln_g), temp_bytes=_score_temp_bytes(Q_TILE, S), name="diff_attn",
    )


NAT_ROWS = 4
NAT_WIN = 12
NEG_BIG = -1e30


def _nat_window_start(g, rows):
    return jnp.clip(NAT_ROWS * g - WIN_R // 2, 0, rows - NAT_WIN)


def _natten_bias(rpb, rows):
    heads = rpb.shape[0]
    groups = rows // NAT_ROWS
    c = np.arange(GRID_W)
    cs = np.clip(c - WIN_C // 2, 0, GRID_W - WIN_C)
    col_mask = (c[None, :] >= cs[:, None]) & (c[None, :] < cs[:, None] + WIN_C)
    col_idx = np.clip(c[None, :] - c[:, None], -(WIN_C - 1), WIN_C - 1) + (WIN_C - 1)
    onehot = (col_idx[None] == np.arange(2 * WIN_C - 1)[:, None, None]).astype(np.float32)
    colbias = jnp.einsum("hrj,jqk->hqrk", rpb.astype(F32), jnp.asarray(onehot), precision=lax.Precision.HIGHEST)
    colbias = jnp.where(jnp.asarray(col_mask)[None, :, None], colbias * LOG2E, NEG_BIG)
    slabs = []
    for g in (0, 1, groups - 1):
        win0 = int(np.clip(NAT_ROWS * g - WIN_R // 2, 0, rows - NAT_WIN))
        for a in range(NAT_ROWS):
            r = NAT_ROWS * g + a
            r0 = int(np.clip(r - WIN_R // 2, 0, rows - WIN_R))
            first = r0 - r + WIN_R - 1
            before = r0 - win0
            slabs.append(jnp.pad(colbias[:, :, first:first + WIN_R], ((0, 0), (0, 0), (before, NAT_WIN - WIN_R - before), (0, 0)),
                                 constant_values=NEG_BIG))
    return jnp.stack(slabs, axis=1).reshape(heads, 3, NAT_ROWS * GRID_W, NAT_WIN * GRID_W)


def _natten_kernel(q_ref, k_ref, v_ref, b_ref, o_ref, vx_ref, *, rows):
    _stage_values(v_ref[0, 0], vx_ref)
    qt = NAT_ROWS * GRID_W
    win = NAT_WIN * GRID_W
    lo = _low_lanes(qt)

    def body(g, carry):
        win0 = _nat_window_start(g, rows)
        variant = lax.shift_right_logical(NAT_ROWS * g - win0, int(math.log2(NAT_ROWS)))
        qstart = pl.multiple_of(g * qt, qt)
        kstart = pl.multiple_of(win0 * GRID_W, GRID_W)
        q = q_ref[0, 0, pl.ds(qstart, qt), :]
        ks = k_ref[0, 0, pl.ds(kstart, win), :]
        vs = vx_ref[pl.ds(kstart, win), :]
        zero = jnp.zeros_like(q)
        outs = []
        for hd in range(2):
            qm = jnp.where(lo, q, zero) if hd == 0 else jnp.where(lo, zero, q)
            outs.append(_weighted_values(_softmax_rows(_scores(qm, ks) + b_ref[hd, variant]), vs))
        o_ref[0, pl.ds(qstart, qt), :] = jnp.where(lo, outs[0], outs[1]).astype(BF16)
        return carry

    lax.fori_loop(0, rows // NAT_ROWS, body, 0, unroll=True)


def _natten(proj, bias, idx):
    B, _, S, _ = proj.shape
    rows = S // GRID_W
    assert NAT_ROWS == WIN_R // 2 and rows % NAT_ROWS == 0 and rows >= NAT_WIN
    base = 3 * A_HEADS
    pairs = B_HEADS // 2
    slab = lambda off: pl.BlockSpec((1, 1, S, LANES), lambda p, b: (b, off + p, 0, 0))
    return _call(
        functools.partial(_natten_kernel, rows=rows),
        grid=(pairs, B),
        in_specs=[
            slab(base), slab(base + pairs), slab(base + 2 * pairs),
            pl.BlockSpec((2, 3, NAT_ROWS * GRID_W, NAT_WIN * GRID_W), lambda p, b: (idx * pairs + p, 0, 0, 0)),
        ],
        out_specs=pl.BlockSpec((1, S, LANES), lambda p, b: (b, 0, p)),
        out_shape=jax.ShapeDtypeStruct((B, S, B_W), BF16),
        scratch_shapes=[pltpu.VMEM((S, 2 * LANES), BF16)],
        args=(proj, proj, proj, bias), temp_bytes=_score_temp_bytes(NAT_ROWS * GRID_W, NAT_WIN * GRID_W), name="natten",
    )


def _mla_proj_kernel(x_ref, g_ref, win_ref, gq_ref, gkv_ref, wuq_ref, wuk_ref, wuv_ref,
                     c_ref, s_ref, q_ref, k_ref, v_ref):
    h = _rms(x_ref[0], g_ref[...]).astype(BF16)
    c, s = c_ref[...], s_ref[...]
    lat = jnp.dot(h, win_ref[...], preferred_element_type=F32)
    cq = _rms(lat[:, :Q_LORA], gq_ref[...]).astype(BF16)
    ckv = _rms(lat[:, Q_LORA:Q_LORA + KV_LORA], gkv_ref[...]).astype(BF16)
    k_rope = _rope_paired(lat[:, Q_LORA + KV_LORA:], c, s)
    scale = (MLA_NOPE + MLA_ROPE) ** -0.5 * LOG2E
    group = 4
    for j0 in range(0, MLA_HEADS, group):
        cols = slice(j0 * LANES, (j0 + group) * LANES)
        qg = jnp.dot(cq, wuq_ref[:, cols], preferred_element_type=F32)
        kg = jnp.dot(ckv, wuk_ref[:, cols], preferred_element_type=F32)
        for jj in range(group):
            sl = slice(jj * LANES, (jj + 1) * LANES)
            q_ref[0, j0 + jj] = (_rope_paired(qg[:, sl], c, s) * scale).astype(BF16)
            k_ref[0, j0 + jj] = (kg[:, sl] + k_rope).astype(BF16)
    vv = jnp.dot(ckv, wuv_ref[...], preferred_element_type=F32)
    for j in range(MLA_PAIRS):
        v_ref[0, j] = vv[:, j * LANES:(j + 1) * LANES].astype(BF16)


def _mla_proj(x, g, w_in, gq, gkv, w_uq, w_uk, w_uv, tables, layer, idx):
    B, S, _ = x.shape
    nt = S // ROW_TILE
    tab_spec = pl.BlockSpec((ROW_TILE, LANES), lambda b, t: (t, 0))
    whole = lambda a: pl.BlockSpec((None,) + a.shape[1:], lambda b, t: (idx,) + (0,) * (a.ndim - 1))
    head_out = lambda n: pl.BlockSpec((1, n, ROW_TILE, LANES), lambda b, t: (b, 0, t, 0))
    temp_bytes = (_nbytes((ROW_TILE, D_MODEL), BF16) + _nbytes((ROW_TILE, w_in.shape[-1]), F32)
                  + 4 * _nbytes((ROW_TILE, 4 * LANES), F32) + _nbytes((ROW_TILE, w_uv.shape[-1]), F32))
    return _call(
        _mla_proj_kernel,
        grid=(B, nt),
        in_specs=[
            pl.BlockSpec((1, ROW_TILE, D_MODEL), lambda b, t: (b, t, 0)),
            pl.BlockSpec((None, 1, D_MODEL), lambda b, t: (layer, 0, 0)),
            whole(w_in), whole(gq), whole(gkv), whole(w_uq), whole(w_uk), whole(w_uv),
            tab_spec, tab_spec,
        ],
        out_specs=[head_out(MLA_HEADS), head_out(MLA_HEADS), head_out(MLA_PAIRS)],
        out_shape=[
            jax.ShapeDtypeStruct((B, MLA_HEADS, S, LANES), BF16),
            jax.ShapeDtypeStruct((B, MLA_HEADS, S, LANES), BF16),
            jax.ShapeDtypeStruct((B, MLA_PAIRS, S, LANES), BF16),
        ],
        args=(x, g, w_in, gq, gkv, w_uq, w_uk, w_uv, *tables), temp_bytes=temp_bytes, name="mla_proj",
    )


def _mla_attn_kernel(q_ref, k_ref, v_ref, o_ref, vx_ref, *, seq):
    _stage_values(v_ref[0, 0], vx_ref)
    lo = _low_lanes(Q_TILE)

    def body(i, carry):
        r = pl.multiple_of(i * Q_TILE, Q_TILE)
        outs = [_weighted_values(_softmax_rows(_scores(q_ref[0, hd, pl.ds(r, Q_TILE), :], k_ref[0, hd])), vx_ref[...])
                for hd in range(2)]
        o_ref[0, pl.ds(r, Q_TILE), :] = jnp.where(lo, outs[0], outs[1]).astype(BF16)
        return carry

    lax.fori_loop(0, seq // Q_TILE, body, 0, unroll=True)


def _mla_attn(q, k, v):
    B, _, S, _ = q.shape
    pair = pl.BlockSpec((1, 2, S, LANES), lambda b, p: (b, p, 0, 0))
    return _call(
        functools.partial(_mla_attn_kernel, seq=S),
        grid=(B, MLA_PAIRS),
        in_specs=[pair, pair, pl.BlockSpec((1, 1, S, LANES), lambda b, p: (b, p, 0, 0))],
        out_specs=pl.BlockSpec((1, S, LANES), lambda b, p: (b, 0, p)),
        out_shape=jax.ShapeDtypeStruct((B, S, MLA_HEADS * MLA_V), BF16),
        scratch_shapes=[pltpu.VMEM((S, 2 * LANES), BF16)],
        args=(q, k, v), temp_bytes=_score_temp_bytes(Q_TILE, S), name="mla_attn",
    )


def _out_mlp_kernel(*refs, n_parts, final):
    x_ref = refs[0]
    part_refs = refs[1:1 + n_parts]
    wo_ref, g_ref, wu_ref, wd_ref = refs[1 + n_parts:5 + n_parts]
    gf_ref = refs[5 + n_parts] if final else None
    o_ref = refs[-1]
    attn = jnp.concatenate([p_ref[0] for p_ref in part_refs], axis=-1)
    x = x_ref[0] + jnp.dot(attn, wo_ref[...], preferred_element_type=F32)
    h = _rms(x, g_ref[...]).astype(BF16)
    acc = x
    for c0 in range(0, D_FF, FF_CHUNK):
        u = jnp.dot(h, wu_ref[:, c0:c0 + FF_CHUNK], preferred_element_type=F32)
        u = jnp.square(jnp.maximum(u, 0.0)).astype(BF16)
        acc = acc + jnp.dot(u, wd_ref[c0:c0 + FF_CHUNK, :], preferred_element_type=F32)
    if final:
        acc = _rms(acc, gf_ref[...])
    o_ref[0] = acc


def _out_mlp(x, parts, w_out, idx, g, w_up, w_down, layer, final_g):
    B, S, _ = x.shape
    nt = S // ROW_TILE
    final = final_g is not None
    row = lambda width: pl.BlockSpec((1, ROW_TILE, width), lambda b, t: (b, t, 0))
    in_specs = [row(D_MODEL)] + [row(p.shape[-1]) for p in parts] + [
        pl.BlockSpec((None, D_MODEL, D_MODEL), lambda b, t: (idx, 0, 0)),
        pl.BlockSpec((None, 1, D_MODEL), lambda b, t: (layer, 0, 0)),
        pl.BlockSpec((None, D_MODEL, D_FF), lambda b, t: (layer, 0, 0)),
        pl.BlockSpec((None, D_FF, D_MODEL), lambda b, t: (layer, 0, 0)),
    ]
    args = [x, *parts, w_out, g, w_up, w_down]
    if final:
        in_specs.append(pl.BlockSpec((1, D_MODEL), lambda b, t: (0, 0)))
        args.append(final_g)
    temp_bytes = (2 * _nbytes((ROW_TILE, D_MODEL), F32) + _nbytes((ROW_TILE, D_MODEL), BF16)
                  + _nbytes((ROW_TILE, FF_CHUNK), F32) + _nbytes((ROW_TILE, FF_CHUNK), BF16))
    return _call(
        functools.partial(_out_mlp_kernel, n_parts=len(parts), final=final),
        grid=(B, nt),
        in_specs=in_specs,
        out_specs=row(D_MODEL),
        out_shape=jax.ShapeDtypeStruct((B, S, D_MODEL), F32),
        args=args, temp_bytes=temp_bytes, name="out_mlp",
    )


def _with_swapped_rope(w):
    half = MLA_ROPE // 2
    return jnp.concatenate([w, w[..., -half:], w[..., -MLA_ROPE:-half]], axis=-1)


def _mla_weight_layouts(od_w_in, od_w_uq, od_w_ukv):
    assert MLA_NOPE + 2 * MLA_ROPE == LANES
    n = od_w_in.shape[0]
    pad_lo = jnp.zeros((n, D_MODEL, MLA_NOPE), F32)
    w_in = jnp.concatenate([od_w_in[:, :, :Q_LORA + KV_LORA], pad_lo, _with_swapped_rope(od_w_in[:, :, Q_LORA + KV_LORA:])], axis=-1)
    w_uq = _with_swapped_rope(od_w_uq.reshape(n, Q_LORA, MLA_HEADS, MLA_NOPE + MLA_ROPE)).reshape(n, Q_LORA, MLA_HEADS * LANES)
    w_ukv = od_w_ukv.reshape(n, KV_LORA, MLA_HEADS, MLA_NOPE + MLA_V)
    w_uk = jnp.pad(w_ukv[..., :MLA_NOPE], ((0, 0), (0, 0), (0, 0), (0, LANES - MLA_NOPE))).reshape(n, KV_LORA, MLA_HEADS * LANES)
    w_uv = w_ukv[..., MLA_NOPE:].reshape(n, KV_LORA, MLA_HEADS * MLA_V)
    return w_in.astype(BF16), w_uq.astype(BF16), w_uk.astype(BF16), w_uv.astype(BF16)


def kernel(x, ln_mix_g, ln_mlp_g, w_up, w_down, ln_f_g, ev_w_in, ev_w_out, ev_lambda_q1, ev_lambda_k1, ev_lambda_q2, ev_lambda_k2, ev_subln_g, ev_rpb, od_w_in, od_q_norm_g, od_kv_norm_g, od_w_uq, od_w_ukv, od_w_out):
    B, S, D = x.shape
    assert D == D_MODEL and S % ROW_TILE == 0 and S % Q_TILE == 0 and S % GRID_W == 0
    x = x.astype(F32)
    ln_mix = ln_mix_g.astype(F32)[:, None, :]
    ln_mlp = ln_mlp_g.astype(F32)[:, None, :]
    w_up_b, w_down_b = w_up.astype(BF16), w_down.astype(BF16)
    ev_w_in_b, ev_w_out_b, od_w_out_b = ev_w_in.astype(BF16), ev_w_out.astype(BF16), od_w_out.astype(BF16)
    od_w_in_b, od_w_uq_b, od_w_uk_b, od_w_uv_b = _mla_weight_layouts(od_w_in.astype(F32), od_w_uq.astype(F32), od_w_ukv.astype(F32))
    gq = od_q_norm_g.astype(F32)[:, None, :]
    gkv = od_kv_norm_g.astype(F32)[:, None, :]
    even_tables = _rope_tables(S)
    mla_tables = _mla_rope_tables(S)
    final_g = ln_f_g.astype(F32)[None, :]
    nat_bias = _natten_bias(ev_rpb.reshape(-1, *ev_rpb.shape[2:]), S // GRID_W)

    for l in range(DEPTH):
        i = l // 2
        last = final_g if l == DEPTH - 1 else None
        if l % 2 == 0:
            lam_init = 0.8 - 0.6 * math.exp(-0.3 * l)
            proj = _even_proj(x, ln_mix, ev_w_in_b, even_tables, l, i)
            lam_params = jnp.stack([ev_lambda_q1[i], ev_lambda_k1[i], ev_lambda_q2[i], ev_lambda_k2[i]]).astype(F32)
            o_a = _diff_attn(proj, lam_params, ev_subln_g[i].astype(F32)[None, :], lam_init)
            o_b = _natten(proj, nat_bias, i)
            parts, w_out, idx = [o_a, o_b], ev_w_out_b, i
        else:
            q, k, v = _mla_proj(x, ln_mix, od_w_in_b, gq, gkv, od_w_uq_b, od_w_uk_b, od_w_uv_b, mla_tables, l, i)
            parts, w_out, idx = [_mla_attn(q, k, v)], od_w_out_b, i
        x = _out_mlp(x, parts, w_out, idx, ln_mlp, w_up_b, w_down_b, l, last)
    return x
```

```python
import functools
import math

import jax
import jax.numpy as jnp
import numpy as np
from jax import lax
from jax.experimental import pallas as pl
from jax.experimental.pallas import tpu as pltpu

D_MODEL = 1024
DEPTH = 4
HEAD_DIM = 64
A_HEADS = D_MODEL // 256
A_V_DIM = 2 * HEAD_DIM
B_HEADS = D_MODEL // 128
ROT_DIM = HEAD_DIM // 4
ROPE_THETA = 500000.0
GRID_W = 64
WIN_R = 8
WIN_C = 16
MLA_HEADS = D_MODEL // 64
MLA_NOPE = 64
MLA_ROPE = 32
MLA_V = 64
Q_LORA = D_MODEL // 4
KV_LORA = D_MODEL // 8
D_FF = 4 * D_MODEL
EPS = 1e-5
A_QK = A_HEADS * 2 * HEAD_DIM
A_V = A_HEADS * A_V_DIM
B_W = B_HEADS * HEAD_DIM
EVEN_IN = 2 * A_QK + A_V + 3 * B_W

LANES = 128
EVEN_SLABS = EVEN_IN // LANES
MLA_PAIRS = MLA_HEADS // 2
V7X_VMEM_BYTES = 64 * 1024 * 1024

ROW_TILE = 512
FF_CHUNK = 1024
Q_TILE = 128
MLA_STEP_PAIRS = 2
EVEN_STEP_SLABS = 2

F32 = jnp.float32
BF16 = jnp.bfloat16
LOG2E = math.log2(math.e)


def _nbytes(shape, dtype):
    return math.prod(1 if d is None else d for d in shape) * jnp.dtype(dtype).itemsize


def _call(body, *, name, grid, in_specs, out_specs, out_shape, args, temp_bytes, scratch_shapes=()):
    outs = out_shape if isinstance(out_shape, (list, tuple)) else [out_shape]
    out_list = out_specs if isinstance(out_specs, (list, tuple)) else [out_specs]
    windows = sum(_nbytes(spec.block_shape, a.dtype) for spec, a in zip([*in_specs, *out_list], [*args, *outs]))
    limit = 2 * windows + sum(_nbytes(m.shape, m.dtype) for m in scratch_shapes) + temp_bytes
    assert limit <= V7X_VMEM_BYTES, (name, limit)
    return pl.pallas_call(
        body, grid=grid, in_specs=in_specs, out_specs=out_specs, out_shape=out_shape, scratch_shapes=list(scratch_shapes),
        compiler_params=pltpu.CompilerParams(vmem_limit_bytes=limit), name=name,
    )(*args)


def _rms(x, g):
    return x * lax.rsqrt(jnp.mean(x * x, axis=-1, keepdims=True) + EPS) * g


def _rope(x, c, sa, sb, half):
    return x * c + pltpu.roll(x, half, 1) * sa + pltpu.roll(x, LANES - half, 1) * sb


def _rope_angles(seq, rot_dim):
    pos = jnp.arange(seq, dtype=F32)
    inv = ROPE_THETA ** (-jnp.arange(0, rot_dim, 2, dtype=F32) / rot_dim)
    ang = pos[:, None] * inv[None, :]
    return jnp.cos(ang), jnp.sin(ang)


def _rope_paired(x, c, s):
    return x * c + pltpu.roll(x, LANES - MLA_ROPE, 1) * s


def _mla_rope_tables(seq):
    cos, sin = _rope_angles(seq, MLA_ROPE)
    ones = jnp.ones((seq, MLA_NOPE), F32)
    zeros = jnp.zeros((seq, MLA_ROPE), F32)
    c = jnp.concatenate([ones, cos, cos, zeros], axis=-1)
    s = jnp.concatenate([jnp.zeros_like(ones), -sin, sin, zeros], axis=-1)
    return c, s


def _rope_tables(seq):
    cos, sin = _rope_angles(seq, ROT_DIM)
    zh = jnp.zeros_like(sin)
    rest = HEAD_DIM - ROT_DIM
    c = jnp.concatenate([cos, cos, jnp.ones((seq, rest), F32)], axis=-1)
    sa = jnp.concatenate([zh, sin, jnp.zeros((seq, rest), F32)], axis=-1)
    sb = jnp.concatenate([-sin, zh, jnp.zeros((seq, rest), F32)], axis=-1)
    return tuple(jnp.tile(t, (1, LANES // HEAD_DIM)) for t in (c, sa, sb))


def _even_proj_kernel(x_ref, g_ref, w_ref, c_ref, sa_ref, sb_ref, o_ref):
    h = _rms(x_ref[0], g_ref[...]).astype(BF16)
    c, sa, sb = c_ref[...], sa_ref[...], sb_ref[...]
    group = 4
    for j0 in range(0, EVEN_SLABS, group):
        p = jnp.dot(h, w_ref[:, j0 * LANES:(j0 + group) * LANES], preferred_element_type=F32)
        for jj in range(group):
            j = j0 + jj
            blk = p[:, jj * LANES:(jj + 1) * LANES]
            if j < 2 * A_HEADS:
                blk = _rope(blk, c, sa, sb, ROT_DIM // 2)
            if j < A_HEADS or 3 * A_HEADS <= j < 4 * A_HEADS:
                blk = blk * (HEAD_DIM ** -0.5 * LOG2E)
            o_ref[0, j] = blk.astype(BF16)


def _even_proj(x, g, w, tables, layer, idx):
    B, S, _ = x.shape
    nt = S // ROW_TILE
    tab_spec = pl.BlockSpec((ROW_TILE, LANES), lambda b, t: (t, 0))
    temp_bytes = _nbytes((ROW_TILE, D_MODEL), BF16) + 4 * _nbytes((ROW_TILE, 4 * LANES), F32)
    return _call(
        _even_proj_kernel,
        grid=(B, nt),
        in_specs=[
            pl.BlockSpec((1, ROW_TILE, D_MODEL), lambda b, t: (b, t, 0)),
            pl.BlockSpec((None, 1, D_MODEL), lambda b, t: (layer, 0, 0)),
            pl.BlockSpec((None, D_MODEL, EVEN_IN), lambda b, t: (idx, 0, 0)),
            tab_spec, tab_spec, tab_spec,
        ],
        out_specs=pl.BlockSpec((1, EVEN_SLABS, ROW_TILE, LANES), lambda b, t: (b, 0, t, 0)),
        out_shape=jax.ShapeDtypeStruct((B, EVEN_SLABS, S, LANES), BF16),
        args=(x, g, w, *tables), temp_bytes=temp_bytes, name="even_proj",
    )


def _softmax_rows(s):
    return jnp.exp2(s - jnp.max(s, axis=-1, keepdims=True)).astype(BF16)


def _scores(q, k):
    return lax.dot_general(q, k, (((1,), (1,)), ((), ())), preferred_element_type=F32)


def _score_temp_bytes(rows, keys):
    return 4 * (_nbytes((rows, keys), F32) + _nbytes((rows, keys), BF16))


def _low_lanes(rows):
    return lax.broadcasted_iota(jnp.int32, (rows, LANES), 1) < HEAD_DIM


def _stage_values(v, vx_ref):
    vx_ref[:, :LANES] = v
    vx_ref[:, LANES:] = jnp.ones_like(v)


def _weighted_values(p, vx):
    res = jnp.dot(p, vx, preferred_element_type=F32)
    return res[:, :LANES] / res[:, LANES:]


def _diff_attn_kernel(q_ref, k_ref, v_ref, lp_ref, g_ref, o_ref, vx_ref, *, lam_init, seq):
    for hd in range(EVEN_STEP_SLABS):
        _stage_values(v_ref[0, hd], vx_ref.at[hd])
    lp = lp_ref[...]
    lam = (jnp.exp(jnp.sum(lp[0:1] * lp[1:2], axis=-1, keepdims=True))
           - jnp.exp(jnp.sum(lp[2:3] * lp[3:4], axis=-1, keepdims=True)) + lam_init)
    lo = _low_lanes(Q_TILE)
    gain = g_ref[...] * (1.0 - lam_init)

    def body(i, carry):
        r = pl.multiple_of(i * Q_TILE, Q_TILE)
        for hd in range(EVEN_STEP_SLABS):
            softmax_pv = lambda q: _weighted_values(_softmax_rows(_scores(q, k_ref[0, hd])), vx_ref[hd])
            q = q_ref[0, hd, pl.ds(r, Q_TILE), :]
            zero = jnp.zeros_like(q)
            o = softmax_pv(jnp.where(lo, q, zero)) - lam * softmax_pv(jnp.where(lo, zero, q))
            o_ref[0, pl.ds(r, Q_TILE), hd * LANES:(hd + 1) * LANES] = _rms(o, gain).astype(BF16)
        return carry

    lax.fori_loop(0, seq // Q_TILE, body, 0, unroll=True)


def _diff_attn(proj, lam_params, subln_g, lam_init):
    B, _, S, _ = proj.shape
    n = EVEN_STEP_SLABS
    slab = lambda off: pl.BlockSpec((1, n, S, LANES), lambda b, h: (b, off // n + h, 0, 0))
    return _call(
        functools.partial(_diff_attn_kernel, lam_init=lam_init, seq=S),
        grid=(B, A_HEADS // n),
        in_specs=[
            slab(0), slab(A_HEADS), slab(2 * A_HEADS),
            pl.BlockSpec((4, HEAD_DIM), lambda b, h: (0, 0)),
            pl.BlockSpec((1, A_V_DIM), lambda b, h: (0, 0)),
        ],
        out_specs=pl.BlockSpec((1, S, n * LANES), lambda b, h: (b, 0, h)),
        out_shape=jax.ShapeDtypeStruct((B, S, A_V), BF16),
        scratch_shapes=[pltpu.VMEM((n, S, 2 * LANES), BF16)],
        args=(proj, proj, proj, lam_params, subln_g), temp_bytes=_score_temp_bytes(Q_TILE, S), name="diff_attn",
    )


NAT_ROWS = 4
NAT_WIN = 12
NEG_BIG = -1e30


def _nat_window_start(g, rows):
    return jnp.clip(NAT_ROWS * g - WIN_R // 2, 0, rows - NAT_WIN)


def _natten_bias(rpb, rows):
    heads = rpb.shape[0]
    groups = rows // NAT_ROWS
    c = np.arange(GRID_W)
    cs = np.clip(c - WIN_C // 2, 0, GRID_W - WIN_C)
    col_mask = (c[None, :] >= cs[:, None]) & (c[None, :] < cs[:, None] + WIN_C)
    col_idx = np.clip(c[None, :] - c[:, None], -(WIN_C - 1), WIN_C - 1) + (WIN_C - 1)
    onehot = (col_idx[None] == np.arange(2 * WIN_C - 1)[:, None, None]).astype(np.float32)
    colbias = jnp.einsum("hrj,jqk->hqrk", rpb.astype(F32), jnp.asarray(onehot), precision=lax.Precision.HIGHEST)
    colbias = jnp.where(jnp.asarray(col_mask)[None, :, None], colbias * LOG2E, NEG_BIG)
    slabs = []
    for g in (0, 1, groups - 1):
        win0 = int(np.clip(NAT_ROWS * g - WIN_R // 2, 0, rows - NAT_WIN))
        for a in range(NAT_ROWS):
            r = NAT_ROWS * g + a
            r0 = int(np.clip(r - WIN_R // 2, 0, rows - WIN_R))
            first = r0 - r + WIN_R - 1
            before = r0 - win0
            slabs.append(jnp.pad(colbias[:, :, first:first + WIN_R], ((0, 0), (0, 0), (before, NAT_WIN - WIN_R - before), (0, 0)),
                                 constant_values=NEG_BIG))
    return jnp.stack(slabs, axis=1).reshape(heads, 3, NAT_ROWS * GRID_W, NAT_WIN * GRID_W)


def _natten_kernel(q_ref, k_ref, v_ref, b_ref, o_ref, vx_ref, *, rows):
    for pr in range(EVEN_STEP_SLABS):
        _stage_values(v_ref[0, pr], vx_ref.at[pr])
    qt = NAT_ROWS * GRID_W
    win = NAT_WIN * GRID_W
    lo = _low_lanes(qt)

    def body(g, carry):
        win0 = _nat_window_start(g, rows)
        variant = lax.shift_right_logical(NAT_ROWS * g - win0, int(math.log2(NAT_ROWS)))
        qstart = pl.multiple_of(g * qt, qt)
        kstart = pl.multiple_of(win0 * GRID_W, GRID_W)
        for pr in range(EVEN_STEP_SLABS):
            q = q_ref[0, pr, pl.ds(qstart, qt), :]
            ks = k_ref[0, pr, pl.ds(kstart, win), :]
            vs = vx_ref[pr, pl.ds(kstart, win), :]
            zero = jnp.zeros_like(q)
            outs = []
            for hd in range(2):
                qm = jnp.where(lo, q, zero) if hd == 0 else jnp.where(lo, zero, q)
                outs.append(_weighted_values(_softmax_rows(_scores(qm, ks) + b_ref[2 * pr + hd, variant]), vs))
            o_ref[0, pl.ds(qstart, qt), pr * LANES:(pr + 1) * LANES] = jnp.where(lo, outs[0], outs[1]).astype(BF16)
        return carry

    lax.fori_loop(0, rows // NAT_ROWS, body, 0, unroll=True)


def _natten(proj, bias, idx):
    B, _, S, _ = proj.shape
    rows = S // GRID_W
    assert NAT_ROWS == WIN_R // 2 and rows % NAT_ROWS == 0 and rows >= NAT_WIN
    base = 3 * A_HEADS
    pairs = B_HEADS // 2
    n = EVEN_STEP_SLABS
    steps = pairs // n
    slab = lambda off: pl.BlockSpec((1, n, S, LANES), lambda p, b: (b, off // n + p, 0, 0))
    return _call(
        functools.partial(_natten_kernel, rows=rows),
        grid=(steps, B),
        in_specs=[
            slab(base), slab(base + pairs), slab(base + 2 * pairs),
            pl.BlockSpec((2 * n, 3, NAT_ROWS * GRID_W, NAT_WIN * GRID_W), lambda p, b: (idx * steps + p, 0, 0, 0)),
        ],
        out_specs=pl.BlockSpec((1, S, n * LANES), lambda p, b: (b, 0, p)),
        out_shape=jax.ShapeDtypeStruct((B, S, B_W), BF16),
        scratch_shapes=[pltpu.VMEM((n, S, 2 * LANES), BF16)],
        args=(proj, proj, proj, bias), temp_bytes=_score_temp_bytes(NAT_ROWS * GRID_W, NAT_WIN * GRID_W), name="natten",
    )


def _mla_proj_kernel(x_ref, g_ref, win_ref, gq_ref, gkv_ref, wuq_ref, wuk_ref, wuv_ref,
                     c_ref, s_ref, q_ref, k_ref, v_ref):
    h = _rms(x_ref[0], g_ref[...]).astype(BF16)
    c, s = c_ref[...], s_ref[...]
    lat = jnp.dot(h, win_ref[...], preferred_element_type=F32)
    cq = _rms(lat[:, :Q_LORA], gq_ref[...]).astype(BF16)
    ckv = _rms(lat[:, Q_LORA:Q_LORA + KV_LORA], gkv_ref[...]).astype(BF16)
    k_rope = _rope_paired(lat[:, Q_LORA + KV_LORA:], c, s)
    scale = (MLA_NOPE + MLA_ROPE) ** -0.5 * LOG2E
    group = 4
    for j0 in range(0, MLA_HEADS, group):
        cols = slice(j0 * LANES, (j0 + group) * LANES)
        qg = jnp.dot(cq, wuq_ref[:, cols], preferred_element_type=F32)
        kg = jnp.dot(ckv, wuk_ref[:, cols], preferred_element_type=F32)
        for jj in range(group):
            sl = slice(jj * LANES, (jj + 1) * LANES)
            q_ref[0, j0 + jj] = (_rope_paired(qg[:, sl], c, s) * scale).astype(BF16)
            k_ref[0, j0 + jj] = (kg[:, sl] + k_rope).astype(BF16)
    vv = jnp.dot(ckv, wuv_ref[...], preferred_element_type=F32)
    for j in range(MLA_PAIRS):
        v_ref[0, j] = vv[:, j * LANES:(j + 1) * LANES].astype(BF16)


def _mla_proj(x, g, w_in, gq, gkv, w_uq, w_uk, w_uv, tables, layer, idx):
    B, S, _ = x.shape
    nt = S // ROW_TILE
    tab_spec = pl.BlockSpec((ROW_TILE, LANES), lambda b, t: (t, 0))
    whole = lambda a: pl.BlockSpec((None,) + a.shape[1:], lambda b, t: (idx,) + (0,) * (a.ndim - 1))
    head_out = lambda n: pl.BlockSpec((1, n, ROW_TILE, LANES), lambda b, t: (b, 0, t, 0))
    temp_bytes = (_nbytes((ROW_TILE, D_MODEL), BF16) + _nbytes((ROW_TILE, w_in.shape[-1]), F32)
                  + 4 * _nbytes((ROW_TILE, 4 * LANES), F32) + _nbytes((ROW_TILE, w_uv.shape[-1]), F32))
    return _call(
        _mla_proj_kernel,
        grid=(B, nt),
        in_specs=[
            pl.BlockSpec((1, ROW_TILE, D_MODEL), lambda b, t: (b, t, 0)),
            pl.BlockSpec((None, 1, D_MODEL), lambda b, t: (layer, 0, 0)),
            whole(w_in), whole(gq), whole(gkv), whole(w_uq), whole(w_uk), whole(w_uv),
            tab_spec, tab_spec,
        ],
        out_specs=[head_out(MLA_HEADS), head_out(MLA_HEADS), head_out(MLA_PAIRS)],
        out_shape=[
            jax.ShapeDtypeStruct((B, MLA_HEADS, S, LANES), BF16),
            jax.ShapeDtypeStruct((B, MLA_HEADS, S, LANES), BF16),
            jax.ShapeDtypeStruct((B, MLA_PAIRS, S, LANES), BF16),
        ],
        args=(x, g, w_in, gq, gkv, w_uq, w_uk, w_uv, *tables), temp_bytes=temp_bytes, name="mla_proj",
    )


def _mla_attn_kernel(q_ref, k_ref, v_ref, o_ref, vx_ref, *, seq):
    for pr in range(MLA_STEP_PAIRS):
        _stage_values(v_ref[0, pr], vx_ref.at[pr])
    lo = _low_lanes(Q_TILE)

    def body(i, carry):
        r = pl.multiple_of(i * Q_TILE, Q_TILE)
        for pr in range(MLA_STEP_PAIRS):
            outs = [_weighted_values(_softmax_rows(_scores(q_ref[0, 2 * pr + hd, pl.ds(r, Q_TILE), :], k_ref[0, 2 * pr + hd])),
                                     vx_ref[pr]) for hd in range(2)]
            o_ref[0, pl.ds(r, Q_TILE), pr * LANES:(pr + 1) * LANES] = jnp.where(lo, outs[0], outs[1]).astype(BF16)
        return carry

    lax.fori_loop(0, seq // Q_TILE, body, 0, unroll=True)


def _mla_attn(q, k, v):
    B, _, S, _ = q.shape
    n = MLA_STEP_PAIRS
    heads = pl.BlockSpec((1, 2 * n, S, LANES), lambda b, p: (b, p, 0, 0))
    return _call(
        functools.partial(_mla_attn_kernel, seq=S),
        grid=(B, MLA_PAIRS // n),
        in_specs=[heads, heads, pl.BlockSpec((1, n, S, LANES), lambda b, p: (b, p, 0, 0))],
        out_specs=pl.BlockSpec((1, S, n * LANES), lambda b, p: (b, 0, p)),
        out_shape=jax.ShapeDtypeStruct((B, S, MLA_HEADS * MLA_V), BF16),
        scratch_shapes=[pltpu.VMEM((n, S, 2 * LANES), BF16)],
        args=(q, k, v), temp_bytes=_score_temp_bytes(Q_TILE, S), name="mla_attn",
    )


def _out_mlp_kernel(*refs, n_parts, final):
    x_ref = refs[0]
    part_refs = refs[1:1 + n_parts]
    wo_ref, g_ref, wu_ref, wd_ref = refs[1 + n_parts:5 + n_parts]
    gf_ref = refs[5 + n_parts] if final else None
    o_ref = refs[-1]
    attn = jnp.concatenate([p_ref[0] for p_ref in part_refs], axis=-1)
    x = x_ref[0] + jnp.dot(attn, wo_ref[...], preferred_element_type=F32)
    h = _rms(x, g_ref[...]).astype(BF16)
    acc = x
    for c0 in range(0, D_FF, FF_CHUNK):
        u = jnp.dot(h, wu_ref[:, c0:c0 + FF_CHUNK], preferred_element_type=F32)
        u = jnp.square(jnp.maximum(u, 0.0)).astype(BF16)
        acc = acc + jnp.dot(u, wd_ref[c0:c0 + FF_CHUNK, :], preferred_element_type=F32)
    if final:
        acc = _rms(acc, gf_ref[...])
    o_ref[0] = acc


def _out_mlp(x, parts, w_out, idx, g, w_up, w_down, layer, final_g):
    B, S, _ = x.shape
    nt = S // ROW_TILE
    final = final_g is not None
    row = lambda width: pl.BlockSpec((1, ROW_TILE, width), lambda b, t: (b, t, 0))
    in_specs = [row(D_MODEL)] + [row(p.shape[-1]) for p in parts] + [
        pl.BlockSpec((None, D_MODEL, D_MODEL), lambda b, t: (idx, 0, 0)),
        pl.BlockSpec((None, 1, D_MODEL), lambda b, t: (layer, 0, 0)),
        pl.BlockSpec((None, D_MODEL, D_FF), lambda b, t: (layer, 0, 0)),
        pl.BlockSpec((None, D_FF, D_MODEL), lambda b, t: (layer, 0, 0)),
    ]
    args = [x, *parts, w_out, g, w_up, w_down]
    if final:
        in_specs.append(pl.BlockSpec((1, D_MODEL), lambda b, t: (0, 0)))
        args.append(final_g)
    temp_bytes = (2 * _nbytes((ROW_TILE, D_MODEL), F32) + _nbytes((ROW_TILE, D_MODEL), BF16)
                  + _nbytes((ROW_TILE, FF_CHUNK), F32) + _nbytes((ROW_TILE, FF_CHUNK), BF16))
    return _call(
        functools.partial(_out_mlp_kernel, n_parts=len(parts), final=final),
        grid=(B, nt),
        in_specs=in_specs,
        out_specs=row(D_MODEL),
        out_shape=jax.ShapeDtypeStruct((B, S, D_MODEL), F32),
        args=args, temp_bytes=temp_bytes, name="out_mlp",
    )


def _with_swapped_rope(w):
    half = MLA_ROPE // 2
    return jnp.concatenate([w, w[..., -half:], w[..., -MLA_ROPE:-half]], axis=-1)


def _mla_weight_layouts(od_w_in, od_w_uq, od_w_ukv):
    assert MLA_NOPE + 2 * MLA_ROPE == LANES
    n = od_w_in.shape[0]
    pad_lo = jnp.zeros((n, D_MODEL, MLA_NOPE), F32)
    w_in = jnp.concatenate([od_w_in[:, :, :Q_LORA + KV_LORA], pad_lo, _with_swapped_rope(od_w_in[:, :, Q_LORA + KV_LORA:])], axis=-1)
    w_uq = _with_swapped_rope(od_w_uq.reshape(n, Q_LORA, MLA_HEADS, MLA_NOPE + MLA_ROPE)).reshape(n, Q_LORA, MLA_HEADS * LANES)
    w_ukv = od_w_ukv.reshape(n, KV_LORA, MLA_HEADS, MLA_NOPE + MLA_V)
    w_uk = jnp.pad(w_ukv[..., :MLA_NOPE], ((0, 0), (0, 0), (0, 0), (0, LANES - MLA_NOPE))).reshape(n, KV_LORA, MLA_HEADS * LANES)
    w_uv = w_ukv[..., MLA_NOPE:].reshape(n, KV_LORA, MLA_HEADS * MLA_V)
    return w_in.astype(BF16), w_uq.astype(BF16), w_uk.astype(BF16), w_uv.astype(BF16)


def kernel(x, ln_mix_g, ln_mlp_g, w_up, w_down, ln_f_g, ev_w_in, ev_w_out, ev_lambda_q1, ev_lambda_k1, ev_lambda_q2, ev_lambda_k2, ev_subln_g, ev_rpb, od_w_in, od_q_norm_g, od_kv_norm_g, od_w_uq, od_w_ukv, od_w_out):
    B, S, D = x.shape
    assert D == D_MODEL and S % ROW_TILE == 0 and S % Q_TILE == 0 and S % GRID_W == 0
    x = x.astype(F32)
    ln_mix = ln_mix_g.astype(F32)[:, None, :]
    ln_mlp = ln_mlp_g.astype(F32)[:, None, :]
    w_up_b, w_down_b = w_up.astype(BF16), w_down.astype(BF16)
    ev_w_in_b, ev_w_out_b, od_w_out_b = ev_w_in.astype(BF16), ev_w_out.astype(BF16), od_w_out.astype(BF16)
    od_w_in_b, od_w_uq_b, od_w_uk_b, od_w_uv_b = _mla_weight_layouts(od_w_in.astype(F32), od_w_uq.astype(F32), od_w_ukv.astype(F32))
    gq = od_q_norm_g.astype(F32)[:, None, :]
    gkv = od_kv_norm_g.astype(F32)[:, None, :]
    even_tables = _rope_tables(S)
    mla_tables = _mla_rope_tables(S)
    final_g = ln_f_g.astype(F32)[None, :]
    nat_bias = _natten_bias(ev_rpb.reshape(-1, *ev_rpb.shape[2:]), S // GRID_W)

    for l in range(DEPTH):
        i = l // 2
        last = final_g if l == DEPTH - 1 else None
        if l % 2 == 0:
            lam_init = 0.8 - 0.6 * math.exp(-0.3 * l)
            proj = _even_proj(x, ln_mix, ev_w_in_b, even_tables, l, i)
            lam_params = jnp.stack([ev_lambda_q1[i], ev_lambda_k1[i], ev_lambda_q2[i], ev_lambda_k2[i]]).astype(F32)
            o_a = _diff_attn(proj, lam_params, ev_subln_g[i].astype(F32)[None, :], lam_init)
            o_b = _natten(proj, nat_bias, i)
            parts, w_out, idx = [o_a, o_b], ev_w_out_b, i
        else:
            q, k, v = _mla_proj(x, ln_mix, od_w_in_b, gq, gkv, od_w_uq_b, od_w_uk_b, od_w_uv_b, mla_tables, l, i)
            parts, w_out, idx = [_mla_attn(q, k, v)], od_w_out_b, i
        x = _out_mlp(x, parts, w_out, idx, ln_mlp, w_up_b, w_down_b, l, last)
    return x
```

```python
import functools
import math

import jax
import jax.numpy as jnp
import numpy as np
from jax import lax
from jax.experimental import pallas as pl
from jax.experimental.pallas import tpu as pltpu

D_MODEL = 1024
DEPTH = 4
HEAD_DIM = 64
A_HEADS = D_MODEL // 256
A_V_DIM = 2 * HEAD_DIM
B_HEADS = D_MODEL // 128
ROT_DIM = HEAD_DIM // 4
ROPE_THETA = 500000.0
GRID_W = 64
WIN_R = 8
WIN_C = 16
MLA_HEADS = D_MODEL // 64
MLA_NOPE = 64
MLA_ROPE = 32
MLA_V = 64
Q_LORA = D_MODEL // 4
KV_LORA = D_MODEL // 8
D_FF = 4 * D_MODEL
EPS = 1e-5
A_QK = A_HEADS * 2 * HEAD_DIM
A_V = A_HEADS * A_V_DIM
B_W = B_HEADS * HEAD_DIM
EVEN_IN = 2 * A_QK + A_V + 3 * B_W

LANES = 128
EVEN_SLABS = EVEN_IN // LANES
MLA_PAIRS = MLA_HEADS // 2
V7X_VMEM_BYTES = 64 * 1024 * 1024

ROW_TILE = 512
MLP_ROW_TILE = 1024
FF_CHUNK = 1024
Q_TILE = 128
MLA_STEP_PAIRS = 2
EVEN_STEP_SLABS = 2

F32 = jnp.float32
BF16 = jnp.bfloat16
LOG2E = math.log2(math.e)


def _nbytes(shape, dtype):
    return math.prod(1 if d is None else d for d in shape) * jnp.dtype(dtype).itemsize


def _call(body, *, name, grid, in_specs, out_specs, out_shape, args, temp_bytes, scratch_shapes=(), resident=()):
    outs = out_shape if isinstance(out_shape, (list, tuple)) else [out_shape]
    out_list = out_specs if isinstance(out_specs, (list, tuple)) else [out_specs]
    sizes = [_nbytes(spec.block_shape, a.dtype) for spec, a in zip([*in_specs, *out_list], [*args, *outs])]
    windows = sum(size * (1 if i in resident else 2) for i, size in enumerate(sizes))
    limit = windows + sum(_nbytes(m.shape, m.dtype) for m in scratch_shapes) + temp_bytes
    assert limit <= V7X_VMEM_BYTES, (name, limit)
    return pl.pallas_call(
        body, grid=grid, in_specs=in_specs, out_specs=out_specs, out_shape=out_shape, scratch_shapes=list(scratch_shapes),
        compiler_params=pltpu.CompilerParams(vmem_limit_bytes=limit), name=name,
    )(*args)


def _rms(x, g):
    return x * lax.rsqrt(jnp.mean(x * x, axis=-1, keepdims=True) + EPS) * g


def _rope(x, c, sa, sb, half):
    return x * c + pltpu.roll(x, half, 1) * sa + pltpu.roll(x, LANES - half, 1) * sb


def _rope_angles(seq, rot_dim):
    pos = jnp.arange(seq, dtype=F32)
    inv = ROPE_THETA ** (-jnp.arange(0, rot_dim, 2, dtype=F32) / rot_dim)
    ang = pos[:, None] * inv[None, :]
    return jnp.cos(ang), jnp.sin(ang)


def _rope_paired(x, c, s):
    return x * c + pltpu.roll(x, LANES - MLA_ROPE, 1) * s


def _mla_rope_tables(seq):
    cos, sin = _rope_angles(seq, MLA_ROPE)
    ones = jnp.ones((seq, MLA_NOPE), F32)
    zeros = jnp.zeros((seq, MLA_ROPE), F32)
    c = jnp.concatenate([ones, cos, cos, zeros], axis=-1)
    s = jnp.concatenate([jnp.zeros_like(ones), -sin, sin, zeros], axis=-1)
    return c, s


def _rope_tables(seq):
    cos, sin = _rope_angles(seq, ROT_DIM)
    zh = jnp.zeros_like(sin)
    rest = HEAD_DIM - ROT_DIM
    c = jnp.concatenate([cos, cos, jnp.ones((seq, rest), F32)], axis=-1)
    sa = jnp.concatenate([zh, sin, jnp.zeros((seq, rest), F32)], axis=-1)
    sb = jnp.concatenate([-sin, zh, jnp.zeros((seq, rest), F32)], axis=-1)
    return tuple(jnp.tile(t, (1, LANES // HEAD_DIM)) for t in (c, sa, sb))


def _even_proj_kernel(x_ref, g_ref, w_ref, c_ref, sa_ref, sb_ref, o_ref):
    h = _rms(x_ref[0], g_ref[...]).astype(BF16)
    c, sa, sb = c_ref[...], sa_ref[...], sb_ref[...]
    group = 4
    for j0 in range(0, EVEN_SLABS, group):
        p = jnp.dot(h, w_ref[:, j0 * LANES:(j0 + group) * LANES], preferred_element_type=F32)
        for jj in range(group):
            j = j0 + jj
            blk = p[:, jj * LANES:(jj + 1) * LANES]
            if j < 2 * A_HEADS:
                blk = _rope(blk, c, sa, sb, ROT_DIM // 2)
            if j < A_HEADS or 3 * A_HEADS <= j < 4 * A_HEADS:
                blk = blk * (HEAD_DIM ** -0.5 * LOG2E)
            o_ref[0, j] = blk.astype(BF16)


def _even_proj(x, g, w, tables, layer, idx):
    B, S, _ = x.shape
    nt = S // ROW_TILE
    tab_spec = pl.BlockSpec((ROW_TILE, LANES), lambda b, t: (t, 0))
    temp_bytes = _nbytes((ROW_TILE, D_MODEL), BF16) + 4 * _nbytes((ROW_TILE, 4 * LANES), F32)
    return _call(
        _even_proj_kernel,
        grid=(B, nt),
        in_specs=[
            pl.BlockSpec((1, ROW_TILE, D_MODEL), lambda b, t: (b, t, 0)),
            pl.BlockSpec((None, 1, D_MODEL), lambda b, t: (layer, 0, 0)),
            pl.BlockSpec((None, D_MODEL, EVEN_IN), lambda b, t: (idx, 0, 0)),
            tab_spec, tab_spec, tab_spec,
        ],
        out_specs=pl.BlockSpec((1, EVEN_SLABS, ROW_TILE, LANES), lambda b, t: (b, 0, t, 0)),
        out_shape=jax.ShapeDtypeStruct((B, EVEN_SLABS, S, LANES), BF16),
        args=(x, g, w, *tables), temp_bytes=temp_bytes, name="even_proj",
    )


def _softmax_rows(s):
    return jnp.exp2(s - jnp.max(s, axis=-1, keepdims=True)).astype(BF16)


def _scores(q, k):
    return lax.dot_general(q, k, (((1,), (1,)), ((), ())), preferred_element_type=F32)


def _score_temp_bytes(rows, keys):
    return 4 * (_nbytes((rows, keys), F32) + _nbytes((rows, keys), BF16))


def _low_lanes(rows):
    return lax.broadcasted_iota(jnp.int32, (rows, LANES), 1) < HEAD_DIM


def _stage_values(v, vx_ref):
    vx_ref[:, :LANES] = v
    vx_ref[:, LANES:] = jnp.ones_like(v)


def _weighted_values(p, vx):
    res = jnp.dot(p, vx, preferred_element_type=F32)
    return res[:, :LANES] / res[:, LANES:]


def _diff_attn_kernel(q_ref, k_ref, v_ref, lp_ref, g_ref, o_ref, vx_ref, *, lam_init, seq):
    for hd in range(EVEN_STEP_SLABS):
        _stage_values(v_ref[0, hd], vx_ref.at[hd])
    lp = lp_ref[...]
    lam = (jnp.exp(jnp.sum(lp[0:1] * lp[1:2], axis=-1, keepdims=True))
           - jnp.exp(jnp.sum(lp[2:3] * lp[3:4], axis=-1, keepdims=True)) + lam_init)
    lo = _low_lanes(Q_TILE)
    gain = g_ref[...] * (1.0 - lam_init)

    def body(i, carry):
        r = pl.multiple_of(i * Q_TILE, Q_TILE)
        for hd in range(EVEN_STEP_SLABS):
            softmax_pv = lambda q: _weighted_values(_softmax_rows(_scores(q, k_ref[0, hd])), vx_ref[hd])
            q = q_ref[0, hd, pl.ds(r, Q_TILE), :]
            zero = jnp.zeros_like(q)
            o = softmax_pv(jnp.where(lo, q, zero)) - lam * softmax_pv(jnp.where(lo, zero, q))
            o_ref[0, pl.ds(r, Q_TILE), hd * LANES:(hd + 1) * LANES] = _rms(o, gain).astype(BF16)
        return carry

    lax.fori_loop(0, seq // Q_TILE, body, 0, unroll=True)


def _diff_attn(proj, lam_params, subln_g, lam_init):
    B, _, S, _ = proj.shape
    n = EVEN_STEP_SLABS
    slab = lambda off: pl.BlockSpec((1, n, S, LANES), lambda b, h: (b, off // n + h, 0, 0))
    return _call(
        functools.partial(_diff_attn_kernel, lam_init=lam_init, seq=S),
        grid=(B, A_HEADS // n),
        in_specs=[
            slab(0), slab(A_HEADS), slab(2 * A_HEADS),
            pl.BlockSpec((4, HEAD_DIM), lambda b, h: (0, 0)),
            pl.BlockSpec((1, A_V_DIM), lambda b, h: (0, 0)),
        ],
        out_specs=pl.BlockSpec((1, S, n * LANES), lambda b, h: (b, 0, h)),
        out_shape=jax.ShapeDtypeStruct((B, S, A_V), BF16),
        scratch_shapes=[pltpu.VMEM((n, S, 2 * LANES), BF16)],
        args=(proj, proj, proj, lam_params, subln_g), temp_bytes=_score_temp_bytes(Q_TILE, S), name="diff_attn",
    )


NAT_ROWS = 4
NAT_WIN = 12
NEG_BIG = -1e30


def _nat_window_start(g, rows):
    return jnp.clip(NAT_ROWS * g - WIN_R // 2, 0, rows - NAT_WIN)


def _natten_bias(rpb, rows):
    heads = rpb.shape[0]
    groups = rows // NAT_ROWS
    c = np.arange(GRID_W)
    cs = np.clip(c - WIN_C // 2, 0, GRID_W - WIN_C)
    col_mask = (c[None, :] >= cs[:, None]) & (c[None, :] < cs[:, None] + WIN_C)
    col_idx = np.clip(c[None, :] - c[:, None], -(WIN_C - 1), WIN_C - 1) + (WIN_C - 1)
    onehot = (col_idx[None] == np.arange(2 * WIN_C - 1)[:, None, None]).astype(np.float32)
    colbias = jnp.einsum("hrj,jqk->hqrk", rpb.astype(F32), jnp.asarray(onehot), precision=lax.Precision.HIGHEST)
    colbias = jnp.where(jnp.asarray(col_mask)[None, :, None], colbias * LOG2E, NEG_BIG)
    slabs = []
    for g in (0, 1, groups - 1):
        win0 = int(np.clip(NAT_ROWS * g - WIN_R // 2, 0, rows - NAT_WIN))
        for a in range(NAT_ROWS):
            r = NAT_ROWS * g + a
            r0 = int(np.clip(r - WIN_R // 2, 0, rows - WIN_R))
            first = r0 - r + WIN_R - 1
            before = r0 - win0
            slabs.append(jnp.pad(colbias[:, :, first:first + WIN_R], ((0, 0), (0, 0), (before, NAT_WIN - WIN_R - before), (0, 0)),
                                 constant_values=NEG_BIG))
    return jnp.stack(slabs, axis=1).reshape(heads, 3, NAT_ROWS * GRID_W, NAT_WIN * GRID_W)


def _natten_kernel(q_ref, k_ref, v_ref, b_ref, o_ref, vx_ref, *, rows):
    for pr in range(EVEN_STEP_SLABS):
        _stage_values(v_ref[0, pr], vx_ref.at[pr])
    qt = NAT_ROWS * GRID_W
    win = NAT_WIN * GRID_W
    lo = _low_lanes(qt)

    def body(g, carry):
        win0 = _nat_window_start(g, rows)
        variant = lax.shift_right_logical(NAT_ROWS * g - win0, int(math.log2(NAT_ROWS)))
        qstart = pl.multiple_of(g * qt, qt)
        kstart = pl.multiple_of(win0 * GRID_W, GRID_W)
        for pr in range(EVEN_STEP_SLABS):
            q = q_ref[0, pr, pl.ds(qstart, qt), :]
            ks = k_ref[0, pr, pl.ds(kstart, win), :]
            vs = vx_ref[pr, pl.ds(kstart, win), :]
            zero = jnp.zeros_like(q)
            outs = []
            for hd in range(2):
                qm = jnp.where(lo, q, zero) if hd == 0 else jnp.where(lo, zero, q)
                outs.append(_weighted_values(_softmax_rows(_scores(qm, ks) + b_ref[2 * pr + hd, variant]), vs))
            o_ref[0, pl.ds(qstart, qt), pr * LANES:(pr + 1) * LANES] = jnp.where(lo, outs[0], outs[1]).astype(BF16)
        return carry

    lax.fori_loop(0, rows // NAT_ROWS, body, 0, unroll=True)


def _natten(proj, bias, idx):
    B, _, S, _ = proj.shape
    rows = S // GRID_W
    assert NAT_ROWS == WIN_R // 2 and rows % NAT_ROWS == 0 and rows >= NAT_WIN
    base = 3 * A_HEADS
    pairs = B_HEADS // 2
    n = EVEN_STEP_SLABS
    steps = pairs // n
    slab = lambda off: pl.BlockSpec((1, n, S, LANES), lambda p, b: (b, off // n + p, 0, 0))
    return _call(
        functools.partial(_natten_kernel, rows=rows),
        grid=(steps, B),
        in_specs=[
            slab(base), slab(base + pairs), slab(base + 2 * pairs),
            pl.BlockSpec((2 * n, 3, NAT_ROWS * GRID_W, NAT_WIN * GRID_W), lambda p, b: (idx * steps + p, 0, 0, 0)),
        ],
        out_specs=pl.BlockSpec((1, S, n * LANES), lambda p, b: (b, 0, p)),
        out_shape=jax.ShapeDtypeStruct((B, S, B_W), BF16),
        scratch_shapes=[pltpu.VMEM((n, S, 2 * LANES), BF16)],
        args=(proj, proj, proj, bias), temp_bytes=_score_temp_bytes(NAT_ROWS * GRID_W, NAT_WIN * GRID_W), name="natten",
    )


def _mla_proj_kernel(x_ref, g_ref, win_ref, gq_ref, gkv_ref, wuq_ref, wuk_ref, wuv_ref,
                     c_ref, s_ref, q_ref, k_ref, v_ref):
    h = _rms(x_ref[0], g_ref[...]).astype(BF16)
    c, s = c_ref[...], s_ref[...]
    lat = jnp.dot(h, win_ref[...], preferred_element_type=F32)
    cq = _rms(lat[:, :Q_LORA], gq_ref[...]).astype(BF16)
    ckv = _rms(lat[:, Q_LORA:Q_LORA + KV_LORA], gkv_ref[...]).astype(BF16)
    k_rope = _rope_paired(lat[:, Q_LORA + KV_LORA:], c, s)
    scale = (MLA_NOPE + MLA_ROPE) ** -0.5 * LOG2E
    group = 4
    for j0 in range(0, MLA_HEADS, group):
        cols = slice(j0 * LANES, (j0 + group) * LANES)
        qg = jnp.dot(cq, wuq_ref[:, cols], preferred_element_type=F32)
        kg = jnp.dot(ckv, wuk_ref[:, cols], preferred_element_type=F32)
        for jj in range(group):
            sl = slice(jj * LANES, (jj + 1) * LANES)
            q_ref[0, j0 + jj] = (_rope_paired(qg[:, sl], c, s) * scale).astype(BF16)
            k_ref[0, j0 + jj] = (kg[:, sl] + k_rope).astype(BF16)
    vv = jnp.dot(ckv, wuv_ref[...], preferred_element_type=F32)
    for j in range(MLA_PAIRS):
        v_ref[0, j] = vv[:, j * LANES:(j + 1) * LANES].astype(BF16)


def _mla_proj(x, g, w_in, gq, gkv, w_uq, w_uk, w_uv, tables, layer, idx):
    B, S, _ = x.shape
    nt = S // ROW_TILE
    tab_spec = pl.BlockSpec((ROW_TILE, LANES), lambda b, t: (t, 0))
    whole = lambda a: pl.BlockSpec((None,) + a.shape[1:], lambda b, t: (idx,) + (0,) * (a.ndim - 1))
    head_out = lambda n: pl.BlockSpec((1, n, ROW_TILE, LANES), lambda b, t: (b, 0, t, 0))
    temp_bytes = (_nbytes((ROW_TILE, D_MODEL), BF16) + _nbytes((ROW_TILE, w_in.shape[-1]), F32)
                  + 4 * _nbytes((ROW_TILE, 4 * LANES), F32) + _nbytes((ROW_TILE, w_uv.shape[-1]), F32))
    return _call(
        _mla_proj_kernel,
        grid=(B, nt),
        in_specs=[
            pl.BlockSpec((1, ROW_TILE, D_MODEL), lambda b, t: (b, t, 0)),
            pl.BlockSpec((None, 1, D_MODEL), lambda b, t: (layer, 0, 0)),
            whole(w_in), whole(gq), whole(gkv), whole(w_uq), whole(w_uk), whole(w_uv),
            tab_spec, tab_spec,
        ],
        out_specs=[head_out(MLA_HEADS), head_out(MLA_HEADS), head_out(MLA_PAIRS)],
        out_shape=[
            jax.ShapeDtypeStruct((B, MLA_HEADS, S, LANES), BF16),
            jax.ShapeDtypeStruct((B, MLA_HEADS, S, LANES), BF16),
            jax.ShapeDtypeStruct((B, MLA_PAIRS, S, LANES), BF16),
        ],
        args=(x, g, w_in, gq, gkv, w_uq, w_uk, w_uv, *tables), temp_bytes=temp_bytes, name="mla_proj",
    )


def _mla_attn_kernel(q_ref, k_ref, v_ref, o_ref, vx_ref, *, seq):
    for pr in range(MLA_STEP_PAIRS):
        _stage_values(v_ref[0, pr], vx_ref.at[pr])
    lo = _low_lanes(Q_TILE)

    def body(i, carry):
        r = pl.multiple_of(i * Q_TILE, Q_TILE)
        for pr in range(MLA_STEP_PAIRS):
            outs = [_weighted_values(_softmax_rows(_scores(q_ref[0, 2 * pr + hd, pl.ds(r, Q_TILE), :], k_ref[0, 2 * pr + hd])),
                                     vx_ref[pr]) for hd in range(2)]
            o_ref[0, pl.ds(r, Q_TILE), pr * LANES:(pr + 1) * LANES] = jnp.where(lo, outs[0], outs[1]).astype(BF16)
        return carry

    lax.fori_loop(0, seq // Q_TILE, body, 0, unroll=True)


def _mla_attn(q, k, v):
    B, _, S, _ = q.shape
    n = MLA_STEP_PAIRS
    heads = pl.BlockSpec((1, 2 * n, S, LANES), lambda b, p: (b, p, 0, 0))
    return _call(
        functools.partial(_mla_attn_kernel, seq=S),
        grid=(B, MLA_PAIRS // n),
        in_specs=[heads, heads, pl.BlockSpec((1, n, S, LANES), lambda b, p: (b, p, 0, 0))],
        out_specs=pl.BlockSpec((1, S, n * LANES), lambda b, p: (b, 0, p)),
        out_shape=jax.ShapeDtypeStruct((B, S, MLA_HEADS * MLA_V), BF16),
        scratch_shapes=[pltpu.VMEM((n, S, 2 * LANES), BF16)],
        args=(q, k, v), temp_bytes=_score_temp_bytes(Q_TILE, S), name="mla_attn",
    )


def _out_mlp_kernel(*refs, n_parts, final):
    x_ref = refs[0]
    part_refs = refs[1:1 + n_parts]
    wo_ref, g_ref, wu_ref, wd_ref = refs[1 + n_parts:5 + n_parts]
    gf_ref = refs[5 + n_parts] if final else None
    o_ref = refs[-1]
    attn = jnp.concatenate([p_ref[0] for p_ref in part_refs], axis=-1)
    x = x_ref[0] + jnp.dot(attn, wo_ref[...], preferred_element_type=F32)
    h = _rms(x, g_ref[...]).astype(BF16)
    acc = x
    for c0 in range(0, D_FF, FF_CHUNK):
        u = jnp.dot(h, wu_ref[:, c0:c0 + FF_CHUNK], preferred_element_type=F32)
        u = jnp.square(jnp.maximum(u, 0.0)).astype(BF16)
        acc = acc + jnp.dot(u, wd_ref[c0:c0 + FF_CHUNK, :], preferred_element_type=F32)
    if final:
        acc = _rms(acc, gf_ref[...])
    o_ref[0] = acc


def _out_mlp(x, parts, w_out, idx, g, w_up, w_down, layer, final_g):
    B, S, _ = x.shape
    nt = S // MLP_ROW_TILE
    final = final_g is not None
    row = lambda width: pl.BlockSpec((1, MLP_ROW_TILE, width), lambda b, t: (b, t, 0))
    in_specs = [row(D_MODEL)] + [row(p.shape[-1]) for p in parts] + [
        pl.BlockSpec((None, D_MODEL, D_MODEL), lambda b, t: (idx, 0, 0)),
        pl.BlockSpec((None, 1, D_MODEL), lambda b, t: (layer, 0, 0)),
        pl.BlockSpec((None, D_MODEL, D_FF), lambda b, t: (layer, 0, 0)),
        pl.BlockSpec((None, D_FF, D_MODEL), lambda b, t: (layer, 0, 0)),
    ]
    args = [x, *parts, w_out, g, w_up, w_down]
    if final:
        in_specs.append(pl.BlockSpec((1, D_MODEL), lambda b, t: (0, 0)))
        args.append(final_g)
    temp_bytes = (2 * _nbytes((MLP_ROW_TILE, D_MODEL), F32) + _nbytes((MLP_ROW_TILE, D_MODEL), BF16)
                  + _nbytes((MLP_ROW_TILE, FF_CHUNK), F32) + _nbytes((MLP_ROW_TILE, FF_CHUNK), BF16))
    weights = range(1 + len(parts), len(args))
    return _call(
        functools.partial(_out_mlp_kernel, n_parts=len(parts), final=final),
        grid=(B, nt),
        in_specs=in_specs,
        out_specs=row(D_MODEL),
        out_shape=jax.ShapeDtypeStruct((B, S, D_MODEL), F32),
        args=args, temp_bytes=temp_bytes, resident=weights, name="out_mlp",
    )


def _with_swapped_rope(w):
    half = MLA_ROPE // 2
    return jnp.concatenate([w, w[..., -half:], w[..., -MLA_ROPE:-half]], axis=-1)


def _mla_weight_layouts(od_w_in, od_w_uq, od_w_ukv):
    assert MLA_NOPE + 2 * MLA_ROPE == LANES
    n = od_w_in.shape[0]
    pad_lo = jnp.zeros((n, D_MODEL, MLA_NOPE), F32)
    w_in = jnp.concatenate([od_w_in[:, :, :Q_LORA + KV_LORA], pad_lo, _with_swapped_rope(od_w_in[:, :, Q_LORA + KV_LORA:])], axis=-1)
    w_uq = _with_swapped_rope(od_w_uq.reshape(n, Q_LORA, MLA_HEADS, MLA_NOPE + MLA_ROPE)).reshape(n, Q_LORA, MLA_HEADS * LANES)
    w_ukv = od_w_ukv.reshape(n, KV_LORA, MLA_HEADS, MLA_NOPE + MLA_V)
    w_uk = jnp.pad(w_ukv[..., :MLA_NOPE], ((0, 0), (0, 0), (0, 0), (0, LANES - MLA_NOPE))).reshape(n, KV_LORA, MLA_HEADS * LANES)
    w_uv = w_ukv[..., MLA_NOPE:].reshape(n, KV_LORA, MLA_HEADS * MLA_V)
    return w_in.astype(BF16), w_uq.astype(BF16), w_uk.astype(BF16), w_uv.astype(BF16)


def kernel(x, ln_mix_g, ln_mlp_g, w_up, w_down, ln_f_g, ev_w_in, ev_w_out, ev_lambda_q1, ev_lambda_k1, ev_lambda_q2, ev_lambda_k2, ev_subln_g, ev_rpb, od_w_in, od_q_norm_g, od_kv_norm_g, od_w_uq, od_w_ukv, od_w_out):
    B, S, D = x.shape
    assert D == D_MODEL and S % ROW_TILE == 0 and S % MLP_ROW_TILE == 0 and S % Q_TILE == 0 and S % GRID_W == 0
    x = x.astype(F32)
    ln_mix = ln_mix_g.astype(F32)[:, None, :]
    ln_mlp = ln_mlp_g.astype(F32)[:, None, :]
    w_up_b, w_down_b = w_up.astype(BF16), w_down.astype(BF16)
    ev_w_in_b, ev_w_out_b, od_w_out_b = ev_w_in.astype(BF16), ev_w_out.astype(BF16), od_w_out.astype(BF16)
    od_w_in_b, od_w_uq_b, od_w_uk_b, od_w_uv_b = _mla_weight_layouts(od_w_in.astype(F32), od_w_uq.astype(F32), od_w_ukv.astype(F32))
    gq = od_q_norm_g.astype(F32)[:, None, :]
    gkv = od_kv_norm_g.astype(F32)[:, None, :]
    even_tables = _rope_tables(S)
    mla_tables = _mla_rope_tables(S)
    final_g = ln_f_g.astype(F32)[None, :]
    nat_bias = _natten_bias(ev_rpb.reshape(-1, *ev_rpb.shape[2:]), S // GRID_W)

    for l in range(DEPTH):
        i = l // 2
        last = final_g if l == DEPTH - 1 else None
        if l % 2 == 0:
            lam_init = 0.8 - 0.6 * math.exp(-0.3 * l)
            proj = _even_proj(x, ln_mix, ev_w_in_b, even_tables, l, i)
            lam_params = jnp.stack([ev_lambda_q1[i], ev_lambda_k1[i], ev_lambda_q2[i], ev_lambda_k2[i]]).astype(F32)
            o_a = _diff_attn(proj, lam_params, ev_subln_g[i].astype(F32)[None, :], lam_init)
            o_b = _natten(proj, nat_bias, i)
            parts, w_out, idx = [o_a, o_b], ev_w_out_b, i
        else:
            q, k, v = _mla_proj(x, ln_mix, od_w_in_b, gq, gkv, od_w_uq_b, od_w_uk_b, od_w_uv_b, mla_tables, l, i)
            parts, w_out, idx = [_mla_attn(q, k, v)], od_w_out_b, i
        x = _out_mlp(x, parts, w_out, idx, ln_mlp, w_up_b, w_down_b, l, last)
    return x
```

```python
import functools
import math

import jax
import jax.numpy as jnp
import numpy as np
from jax import lax
from jax.experimental import pallas as pl
from jax.experimental.pallas import tpu as pltpu

D_MODEL = 1024
DEPTH = 4
HEAD_DIM = 64
A_HEADS = D_MODEL // 256
A_V_DIM = 2 * HEAD_DIM
B_HEADS = D_MODEL // 128
ROT_DIM = HEAD_DIM // 4
ROPE_THETA = 500000.0
GRID_W = 64
WIN_R = 8
WIN_C = 16
MLA_HEADS = D_MODEL // 64
MLA_NOPE = 64
MLA_ROPE = 32
MLA_V = 64
Q_LORA = D_MODEL // 4
KV_LORA = D_MODEL // 8
D_FF = 4 * D_MODEL
EPS = 1e-5
A_QK = A_HEADS * 2 * HEAD_DIM
A_V = A_HEADS * A_V_DIM
B_W = B_HEADS * HEAD_DIM
EVEN_IN = 2 * A_QK + A_V + 3 * B_W

LANES = 128
EVEN_SLABS = EVEN_IN // LANES
MLA_PAIRS = MLA_HEADS // 2
V7X_VMEM_BYTES = 64 * 1024 * 1024

ROW_TILE = 1024
MLP_ROW_TILE = 1024
FF_CHUNK = 1024
Q_TILE = 128
MLA_STEP_PAIRS = 2
EVEN_STEP_SLABS = 2

F32 = jnp.float32
BF16 = jnp.bfloat16
LOG2E = math.log2(math.e)


def _nbytes(shape, dtype):
    return math.prod(1 if d is None else d for d in shape) * jnp.dtype(dtype).itemsize


def _call(body, *, name, grid, in_specs, out_specs, out_shape, args, temp_bytes, scratch_shapes=(), resident=()):
    outs = out_shape if isinstance(out_shape, (list, tuple)) else [out_shape]
    out_list = out_specs if isinstance(out_specs, (list, tuple)) else [out_specs]
    sizes = [_nbytes(spec.block_shape, a.dtype) for spec, a in zip([*in_specs, *out_list], [*args, *outs])]
    windows = sum(size * (1 if i in resident else 2) for i, size in enumerate(sizes))
    limit = windows + sum(_nbytes(m.shape, m.dtype) for m in scratch_shapes) + temp_bytes
    assert limit <= V7X_VMEM_BYTES, (name, limit)
    return pl.pallas_call(
        body, grid=grid, in_specs=in_specs, out_specs=out_specs, out_shape=out_shape, scratch_shapes=list(scratch_shapes),
        compiler_params=pltpu.CompilerParams(vmem_limit_bytes=limit), name=name,
    )(*args)


def _rms(x, g):
    return x * lax.rsqrt(jnp.mean(x * x, axis=-1, keepdims=True) + EPS) * g


def _rope(x, c, sa, sb, half):
    return x * c + pltpu.roll(x, half, 1) * sa + pltpu.roll(x, LANES - half, 1) * sb


def _rope_angles(seq, rot_dim):
    pos = jnp.arange(seq, dtype=F32)
    inv = ROPE_THETA ** (-jnp.arange(0, rot_dim, 2, dtype=F32) / rot_dim)
    ang = pos[:, None] * inv[None, :]
    return jnp.cos(ang), jnp.sin(ang)


def _rope_paired(x, c, s):
    return x * c + pltpu.roll(x, LANES - MLA_ROPE, 1) * s


def _mla_rope_tables(seq):
    cos, sin = _rope_angles(seq, MLA_ROPE)
    ones = jnp.ones((seq, MLA_NOPE), F32)
    zeros = jnp.zeros((seq, MLA_ROPE), F32)
    c = jnp.concatenate([ones, cos, cos, zeros], axis=-1)
    s = jnp.concatenate([jnp.zeros_like(ones), -sin, sin, zeros], axis=-1)
    return c, s


def _rope_tables(seq):
    cos, sin = _rope_angles(seq, ROT_DIM)
    zh = jnp.zeros_like(sin)
    rest = HEAD_DIM - ROT_DIM
    c = jnp.concatenate([cos, cos, jnp.ones((seq, rest), F32)], axis=-1)
    sa = jnp.concatenate([zh, sin, jnp.zeros((seq, rest), F32)], axis=-1)
    sb = jnp.concatenate([-sin, zh, jnp.zeros((seq, rest), F32)], axis=-1)
    return tuple(jnp.tile(t, (1, LANES // HEAD_DIM)) for t in (c, sa, sb))


def _even_proj_kernel(x_ref, g_ref, w_ref, c_ref, sa_ref, sb_ref, o_ref):
    h = _rms(x_ref[0], g_ref[...]).astype(BF16)
    c, sa, sb = c_ref[...], sa_ref[...], sb_ref[...]
    group = 4
    for j0 in range(0, EVEN_SLABS, group):
        p = jnp.dot(h, w_ref[:, j0 * LANES:(j0 + group) * LANES], preferred_element_type=F32)
        for jj in range(group):
            j = j0 + jj
            blk = p[:, jj * LANES:(jj + 1) * LANES]
            if j < 2 * A_HEADS:
                blk = _rope(blk, c, sa, sb, ROT_DIM // 2)
            if j < A_HEADS or 3 * A_HEADS <= j < 4 * A_HEADS:
                blk = blk * (HEAD_DIM ** -0.5 * LOG2E)
            o_ref[0, j] = blk.astype(BF16)


def _even_proj(x, g, w, tables, layer, idx):
    B, S, _ = x.shape
    nt = S // ROW_TILE
    tab_spec = pl.BlockSpec((ROW_TILE, LANES), lambda b, t: (t, 0))
    temp_bytes = _nbytes((ROW_TILE, D_MODEL), BF16) + 4 * _nbytes((ROW_TILE, 4 * LANES), F32)
    return _call(
        _even_proj_kernel,
        grid=(B, nt),
        in_specs=[
            pl.BlockSpec((1, ROW_TILE, D_MODEL), lambda b, t: (b, t, 0)),
            pl.BlockSpec((None, 1, D_MODEL), lambda b, t: (layer, 0, 0)),
            pl.BlockSpec((None, D_MODEL, EVEN_IN), lambda b, t: (idx, 0, 0)),
            tab_spec, tab_spec, tab_spec,
        ],
        out_specs=pl.BlockSpec((1, EVEN_SLABS, ROW_TILE, LANES), lambda b, t: (b, 0, t, 0)),
        out_shape=jax.ShapeDtypeStruct((B, EVEN_SLABS, S, LANES), BF16),
        args=(x, g, w, *tables), temp_bytes=temp_bytes, resident=(1, 2), name="even_proj",
    )


def _softmax_rows(s):
    return jnp.exp2(s - jnp.max(s, axis=-1, keepdims=True)).astype(BF16)


def _scores(q, k):
    return lax.dot_general(q, k, (((1,), (1,)), ((), ())), preferred_element_type=F32)


def _score_temp_bytes(rows, keys):
    return 4 * (_nbytes((rows, keys), F32) + _nbytes((rows, keys), BF16))


def _low_lanes(rows):
    return lax.broadcasted_iota(jnp.int32, (rows, LANES), 1) < HEAD_DIM


def _stage_values(v, vx_ref):
    vx_ref[:, :LANES] = v
    vx_ref[:, LANES:] = jnp.ones_like(v)


def _weighted_values(p, vx):
    res = jnp.dot(p, vx, preferred_element_type=F32)
    return res[:, :LANES] / res[:, LANES:]


def _diff_attn_kernel(q_ref, k_ref, v_ref, lp_ref, g_ref, o_ref, vx_ref, *, lam_init, seq):
    for hd in range(EVEN_STEP_SLABS):
        _stage_values(v_ref[0, hd], vx_ref.at[hd])
    lp = lp_ref[...]
    lam = (jnp.exp(jnp.sum(lp[0:1] * lp[1:2], axis=-1, keepdims=True))
           - jnp.exp(jnp.sum(lp[2:3] * lp[3:4], axis=-1, keepdims=True)) + lam_init)
    lo = _low_lanes(Q_TILE)
    gain = g_ref[...] * (1.0 - lam_init)

    def body(i, carry):
        r = pl.multiple_of(i * Q_TILE, Q_TILE)
        for hd in range(EVEN_STEP_SLABS):
            softmax_pv = lambda q: _weighted_values(_softmax_rows(_scores(q, k_ref[0, hd])), vx_ref[hd])
            q = q_ref[0, hd, pl.ds(r, Q_TILE), :]
            zero = jnp.zeros_like(q)
            o = softmax_pv(jnp.where(lo, q, zero)) - lam * softmax_pv(jnp.where(lo, zero, q))
            o_ref[0, pl.ds(r, Q_TILE), hd * LANES:(hd + 1) * LANES] = _rms(o, gain).astype(BF16)
        return carry

    lax.fori_loop(0, seq // Q_TILE, body, 0, unroll=True)


def _diff_attn(proj, lam_params, subln_g, lam_init):
    B, _, S, _ = proj.shape
    n = EVEN_STEP_SLABS
    slab = lambda off: pl.BlockSpec((1, n, S, LANES), lambda b, h: (b, off // n + h, 0, 0))
    return _call(
        functools.partial(_diff_attn_kernel, lam_init=lam_init, seq=S),
        grid=(B, A_HEADS // n),
        in_specs=[
            slab(0), slab(A_HEADS), slab(2 * A_HEADS),
            pl.BlockSpec((4, HEAD_DIM), lambda b, h: (0, 0)),
            pl.BlockSpec((1, A_V_DIM), lambda b, h: (0, 0)),
        ],
        out_specs=pl.BlockSpec((1, S, n * LANES), lambda b, h: (b, 0, h)),
        out_shape=jax.ShapeDtypeStruct((B, S, A_V), BF16),
        scratch_shapes=[pltpu.VMEM((n, S, 2 * LANES), BF16)],
        args=(proj, proj, proj, lam_params, subln_g), temp_bytes=_score_temp_bytes(Q_TILE, S), name="diff_attn",
    )


NAT_ROWS = 4
NAT_WIN = 12
NEG_BIG = -1e30


def _nat_window_start(g, rows):
    return jnp.clip(NAT_ROWS * g - WIN_R // 2, 0, rows - NAT_WIN)


def _natten_bias(rpb, rows):
    heads = rpb.shape[0]
    groups = rows // NAT_ROWS
    c = np.arange(GRID_W)
    cs = np.clip(c - WIN_C // 2, 0, GRID_W - WIN_C)
    col_mask = (c[None, :] >= cs[:, None]) & (c[None, :] < cs[:, None] + WIN_C)
    col_idx = np.clip(c[None, :] - c[:, None], -(WIN_C - 1), WIN_C - 1) + (WIN_C - 1)
    onehot = (col_idx[None] == np.arange(2 * WIN_C - 1)[:, None, None]).astype(np.float32)
    colbias = jnp.einsum("hrj,jqk->hqrk", rpb.astype(F32), jnp.asarray(onehot), precision=lax.Precision.HIGHEST)
    colbias = jnp.where(jnp.asarray(col_mask)[None, :, None], colbias * LOG2E, NEG_BIG)
    slabs = []
    for g in (0, 1, groups - 1):
        win0 = int(np.clip(NAT_ROWS * g - WIN_R // 2, 0, rows - NAT_WIN))
        for a in range(NAT_ROWS):
            r = NAT_ROWS * g + a
            r0 = int(np.clip(r - WIN_R // 2, 0, rows - WIN_R))
            first = r0 - r + WIN_R - 1
            before = r0 - win0
            slabs.append(jnp.pad(colbias[:, :, first:first + WIN_R], ((0, 0), (0, 0), (before, NAT_WIN - WIN_R - before), (0, 0)),
                                 constant_values=NEG_BIG))
    return jnp.stack(slabs, axis=1).reshape(heads, 3, NAT_ROWS * GRID_W, NAT_WIN * GRID_W)


def _natten_kernel(q_ref, k_ref, v_ref, b_ref, o_ref, vx_ref, *, rows):
    for pr in range(EVEN_STEP_SLABS):
        _stage_values(v_ref[0, pr], vx_ref.at[pr])
    qt = NAT_ROWS * GRID_W
    win = NAT_WIN * GRID_W
    lo = _low_lanes(qt)

    def body(g, carry):
        win0 = _nat_window_start(g, rows)
        variant = lax.shift_right_logical(NAT_ROWS * g - win0, int(math.log2(NAT_ROWS)))
        qstart = pl.multiple_of(g * qt, qt)
        kstart = pl.multiple_of(win0 * GRID_W, GRID_W)
        for pr in range(EVEN_STEP_SLABS):
            q = q_ref[0, pr, pl.ds(qstart, qt), :]
            ks = k_ref[0, pr, pl.ds(kstart, win), :]
            vs = vx_ref[pr, pl.ds(kstart, win), :]
            zero = jnp.zeros_like(q)
            outs = []
            for hd in range(2):
                qm = jnp.where(lo, q, zero) if hd == 0 else jnp.where(lo, zero, q)
                outs.append(_weighted_values(_softmax_rows(_scores(qm, ks) + b_ref[2 * pr + hd, variant]), vs))
            o_ref[0, pl.ds(qstart, qt), pr * LANES:(pr + 1) * LANES] = jnp.where(lo, outs[0], outs[1]).astype(BF16)
        return carry

    lax.fori_loop(0, rows // NAT_ROWS, body, 0, unroll=True)


def _natten(proj, bias, idx):
    B, _, S, _ = proj.shape
    rows = S // GRID_W
    assert NAT_ROWS == WIN_R // 2 and rows % NAT_ROWS == 0 and rows >= NAT_WIN
    base = 3 * A_HEADS
    pairs = B_HEADS // 2
    n = EVEN_STEP_SLABS
    steps = pairs // n
    slab = lambda off: pl.BlockSpec((1, n, S, LANES), lambda p, b: (b, off // n + p, 0, 0))
    return _call(
        functools.partial(_natten_kernel, rows=rows),
        grid=(steps, B),
        in_specs=[
            slab(base), slab(base + pairs), slab(base + 2 * pairs),
            pl.BlockSpec((2 * n, 3, NAT_ROWS * GRID_W, NAT_WIN * GRID_W), lambda p, b: (idx * steps + p, 0, 0, 0)),
        ],
        out_specs=pl.BlockSpec((1, S, n * LANES), lambda p, b: (b, 0, p)),
        out_shape=jax.ShapeDtypeStruct((B, S, B_W), BF16),
        scratch_shapes=[pltpu.VMEM((n, S, 2 * LANES), BF16)],
        args=(proj, proj, proj, bias), temp_bytes=_score_temp_bytes(NAT_ROWS * GRID_W, NAT_WIN * GRID_W), name="natten",
    )


def _mla_proj_kernel(x_ref, g_ref, win_ref, gq_ref, gkv_ref, wuq_ref, wuk_ref, wuv_ref,
                     c_ref, s_ref, q_ref, k_ref, v_ref):
    h = _rms(x_ref[0], g_ref[...]).astype(BF16)
    c, s = c_ref[...], s_ref[...]
    lat = jnp.dot(h, win_ref[...], preferred_element_type=F32)
    cq = _rms(lat[:, :Q_LORA], gq_ref[...]).astype(BF16)
    ckv = _rms(lat[:, Q_LORA:Q_LORA + KV_LORA], gkv_ref[...]).astype(BF16)
    k_rope = _rope_paired(lat[:, Q_LORA + KV_LORA:], c, s)
    scale = (MLA_NOPE + MLA_ROPE) ** -0.5 * LOG2E
    group = 4
    for j0 in range(0, MLA_HEADS, group):
        cols = slice(j0 * LANES, (j0 + group) * LANES)
        qg = jnp.dot(cq, wuq_ref[:, cols], preferred_element_type=F32)
        kg = jnp.dot(ckv, wuk_ref[:, cols], preferred_element_type=F32)
        for jj in range(group):
            sl = slice(jj * LANES, (jj + 1) * LANES)
            q_ref[0, j0 + jj] = (_rope_paired(qg[:, sl], c, s) * scale).astype(BF16)
            k_ref[0, j0 + jj] = (kg[:, sl] + k_rope).astype(BF16)
    vv = jnp.dot(ckv, wuv_ref[...], preferred_element_type=F32)
    for j in range(MLA_PAIRS):
        v_ref[0, j] = vv[:, j * LANES:(j + 1) * LANES].astype(BF16)


def _mla_proj(x, g, w_in, gq, gkv, w_uq, w_uk, w_uv, tables, layer, idx):
    B, S, _ = x.shape
    nt = S // ROW_TILE
    tab_spec = pl.BlockSpec((ROW_TILE, LANES), lambda b, t: (t, 0))
    whole = lambda a: pl.BlockSpec((None,) + a.shape[1:], lambda b, t: (idx,) + (0,) * (a.ndim - 1))
    head_out = lambda n: pl.BlockSpec((1, n, ROW_TILE, LANES), lambda b, t: (b, 0, t, 0))
    temp_bytes = (_nbytes((ROW_TILE, D_MODEL), BF16) + _nbytes((ROW_TILE, w_in.shape[-1]), F32)
                  + 4 * _nbytes((ROW_TILE, 4 * LANES), F32) + _nbytes((ROW_TILE, w_uv.shape[-1]), F32))
    return _call(
        _mla_proj_kernel,
        grid=(B, nt),
        in_specs=[
            pl.BlockSpec((1, ROW_TILE, D_MODEL), lambda b, t: (b, t, 0)),
            pl.BlockSpec((None, 1, D_MODEL), lambda b, t: (layer, 0, 0)),
            whole(w_in), whole(gq), whole(gkv), whole(w_uq), whole(w_uk), whole(w_uv),
            tab_spec, tab_spec,
        ],
        out_specs=[head_out(MLA_HEADS), head_out(MLA_HEADS), head_out(MLA_PAIRS)],
        out_shape=[
            jax.ShapeDtypeStruct((B, MLA_HEADS, S, LANES), BF16),
            jax.ShapeDtypeStruct((B, MLA_HEADS, S, LANES), BF16),
            jax.ShapeDtypeStruct((B, MLA_PAIRS, S, LANES), BF16),
        ],
        args=(x, g, w_in, gq, gkv, w_uq, w_uk, w_uv, *tables), temp_bytes=temp_bytes, resident=range(1, 8), name="mla_proj",
    )


def _mla_attn_kernel(q_ref, k_ref, v_ref, o_ref, vx_ref, *, seq):
    for pr in range(MLA_STEP_PAIRS):
        _stage_values(v_ref[0, pr], vx_ref.at[pr])
    lo = _low_lanes(Q_TILE)

    def body(i, carry):
        r = pl.multiple_of(i * Q_TILE, Q_TILE)
        for pr in range(MLA_STEP_PAIRS):
            outs = [_weighted_values(_softmax_rows(_scores(q_ref[0, 2 * pr + hd, pl.ds(r, Q_TILE), :], k_ref[0, 2 * pr + hd])),
                                     vx_ref[pr]) for hd in range(2)]
            o_ref[0, pl.ds(r, Q_TILE), pr * LANES:(pr + 1) * LANES] = jnp.where(lo, outs[0], outs[1]).astype(BF16)
        return carry

    lax.fori_loop(0, seq // Q_TILE, body, 0, unroll=True)


def _mla_attn(q, k, v):
    B, _, S, _ = q.shape
    n = MLA_STEP_PAIRS
    heads = pl.BlockSpec((1, 2 * n, S, LANES), lambda b, p: (b, p, 0, 0))
    return _call(
        functools.partial(_mla_attn_kernel, seq=S),
        grid=(B, MLA_PAIRS // n),
        in_specs=[heads, heads, pl.BlockSpec((1, n, S, LANES), lambda b, p: (b, p, 0, 0))],
        out_specs=pl.BlockSpec((1, S, n * LANES), lambda b, p: (b, 0, p)),
        out_shape=jax.ShapeDtypeStruct((B, S, MLA_HEADS * MLA_V), BF16),
        scratch_shapes=[pltpu.VMEM((n, S, 2 * LANES), BF16)],
        args=(q, k, v), temp_bytes=_score_temp_bytes(Q_TILE, S), name="mla_attn",
    )


def _out_mlp_kernel(*refs, n_parts, final):
    x_ref = refs[0]
    part_refs = refs[1:1 + n_parts]
    wo_ref, g_ref, wu_ref, wd_ref = refs[1 + n_parts:5 + n_parts]
    gf_ref = refs[5 + n_parts] if final else None
    o_ref = refs[-1]
    attn = jnp.concatenate([p_ref[0] for p_ref in part_refs], axis=-1)
    x = x_ref[0] + jnp.dot(attn, wo_ref[...], preferred_element_type=F32)
    h = _rms(x, g_ref[...]).astype(BF16)
    acc = x
    for c0 in range(0, D_FF, FF_CHUNK):
        u = jnp.dot(h, wu_ref[:, c0:c0 + FF_CHUNK], preferred_element_type=F32)
        u = jnp.square(jnp.maximum(u, 0.0)).astype(BF16)
        acc = acc + jnp.dot(u, wd_ref[c0:c0 + FF_CHUNK, :], preferred_element_type=F32)
    if final:
        acc = _rms(acc, gf_ref[...])
    o_ref[0] = acc


def _out_mlp(x, parts, w_out, idx, g, w_up, w_down, layer, final_g):
    B, S, _ = x.shape
    nt = S // MLP_ROW_TILE
    final = final_g is not None
    row = lambda width: pl.BlockSpec((1, MLP_ROW_TILE, width), lambda b, t: (b, t, 0))
    in_specs = [row(D_MODEL)] + [row(p.shape[-1]) for p in parts] + [
        pl.BlockSpec((None, D_MODEL, D_MODEL), lambda b, t: (idx, 0, 0)),
        pl.BlockSpec((None, 1, D_MODEL), lambda b, t: (layer, 0, 0)),
        pl.BlockSpec((None, D_MODEL, D_FF), lambda b, t: (layer, 0, 0)),
        pl.BlockSpec((None, D_FF, D_MODEL), lambda b, t: (layer, 0, 0)),
    ]
    args = [x, *parts, w_out, g, w_up, w_down]
    if final:
        in_specs.append(pl.BlockSpec((1, D_MODEL), lambda b, t: (0, 0)))
        args.append(final_g)
    temp_bytes = (2 * _nbytes((MLP_ROW_TILE, D_MODEL), F32) + _nbytes((MLP_ROW_TILE, D_MODEL), BF16)
                  + _nbytes((MLP_ROW_TILE, FF_CHUNK), F32) + _nbytes((MLP_ROW_TILE, FF_CHUNK), BF16))
    weights = range(1 + len(parts), len(args))
    return _call(
        functools.partial(_out_mlp_kernel, n_parts=len(parts), final=final),
        grid=(B, nt),
        in_specs=in_specs,
        out_specs=row(D_MODEL),
        out_shape=jax.ShapeDtypeStruct((B, S, D_MODEL), F32),
        args=args, temp_bytes=temp_bytes, resident=weights, name="out_mlp",
    )


def _with_swapped_rope(w):
    half = MLA_ROPE // 2
    return jnp.concatenate([w, w[..., -half:], w[..., -MLA_ROPE:-half]], axis=-1)


def _mla_weight_layouts(od_w_in, od_w_uq, od_w_ukv):
    assert MLA_NOPE + 2 * MLA_ROPE == LANES
    n = od_w_in.shape[0]
    pad_lo = jnp.zeros((n, D_MODEL, MLA_NOPE), F32)
    w_in = jnp.concatenate([od_w_in[:, :, :Q_LORA + KV_LORA], pad_lo, _with_swapped_rope(od_w_in[:, :, Q_LORA + KV_LORA:])], axis=-1)
    w_uq = _with_swapped_rope(od_w_uq.reshape(n, Q_LORA, MLA_HEADS, MLA_NOPE + MLA_ROPE)).reshape(n, Q_LORA, MLA_HEADS * LANES)
    w_ukv = od_w_ukv.reshape(n, KV_LORA, MLA_HEADS, MLA_NOPE + MLA_V)
    w_uk = jnp.pad(w_ukv[..., :MLA_NOPE], ((0, 0), (0, 0), (0, 0), (0, LANES - MLA_NOPE))).reshape(n, KV_LORA, MLA_HEADS * LANES)
    w_uv = w_ukv[..., MLA_NOPE:].reshape(n, KV_LORA, MLA_HEADS * MLA_V)
    return w_in.astype(BF16), w_uq.astype(BF16), w_uk.astype(BF16), w_uv.astype(BF16)


def kernel(x, ln_mix_g, ln_mlp_g, w_up, w_down, ln_f_g, ev_w_in, ev_w_out, ev_lambda_q1, ev_lambda_k1, ev_lambda_q2, ev_lambda_k2, ev_subln_g, ev_rpb, od_w_in, od_q_norm_g, od_kv_norm_g, od_w_uq, od_w_ukv, od_w_out):
    B, S, D = x.shape
    assert D == D_MODEL and S % ROW_TILE == 0 and S % MLP_ROW_TILE == 0 and S % Q_TILE == 0 and S % GRID_W == 0
    x = x.astype(F32)
    ln_mix = ln_mix_g.astype(F32)[:, None, :]
    ln_mlp = ln_mlp_g.astype(F32)[:, None, :]
    w_up_b, w_down_b = w_up.astype(BF16), w_down.astype(BF16)
    ev_w_in_b, ev_w_out_b, od_w_out_b = ev_w_in.astype(BF16), ev_w_out.astype(BF16), od_w_out.astype(BF16)
    od_w_in_b, od_w_uq_b, od_w_uk_b, od_w_uv_b = _mla_weight_layouts(od_w_in.astype(F32), od_w_uq.astype(F32), od_w_ukv.astype(F32))
    gq = od_q_norm_g.astype(F32)[:, None, :]
    gkv = od_kv_norm_g.astype(F32)[:, None, :]
    even_tables = _rope_tables(S)
    mla_tables = _mla_rope_tables(S)
    final_g = ln_f_g.astype(F32)[None, :]
    nat_bias = _natten_bias(ev_rpb.reshape(-1, *ev_rpb.shape[2:]), S // GRID_W)

    for l in range(DEPTH):
        i = l // 2
        last = final_g if l == DEPTH - 1 else None
        if l % 2 == 0:
            lam_init = 0.8 - 0.6 * math.exp(-0.3 * l)
            proj = _even_proj(x, ln_mix, ev_w_in_b, even_tables, l, i)
            lam_params = jnp.stack([ev_lambda_q1[i], ev_lambda_k1[i], ev_lambda_q2[i], ev_lambda_k2[i]]).astype(F32)
            o_a = _diff_attn(proj, lam_params, ev_subln_g[i].astype(F32)[None, :], lam_init)
            o_b = _natten(proj, nat_bias, i)
            parts, w_out, idx = [o_a, o_b], ev_w_out_b, i
        else:
            q, k, v = _mla_proj(x, ln_mix, od_w_in_b, gq, gkv, od_w_uq_b, od_w_uk_b, od_w_uv_b, mla_tables, l, i)
            parts, w_out, idx = [_mla_attn(q, k, v)], od_w_out_b, i
        x = _out_mlp(x, parts, w_out, idx, ln_mlp, w_up_b, w_down_b, l, last)
    return x
```

```python
import functools
import math

import jax
import jax.numpy as jnp
import numpy as np
from jax import lax
from jax.experimental import pallas as pl
from jax.experimental.pallas import tpu as pltpu

D_MODEL = 1024
DEPTH = 4
HEAD_DIM = 64
A_HEADS = D_MODEL // 256
A_V_DIM = 2 * HEAD_DIM
B_HEADS = D_MODEL // 128
ROT_DIM = HEAD_DIM // 4
ROPE_THETA = 500000.0
GRID_W = 64
WIN_R = 8
WIN_C = 16
MLA_HEADS = D_MODEL // 64
MLA_NOPE = 64
MLA_ROPE = 32
MLA_V = 64
Q_LORA = D_MODEL // 4
KV_LORA = D_MODEL // 8
D_FF = 4 * D_MODEL
EPS = 1e-5
A_QK = A_HEADS * 2 * HEAD_DIM
A_V = A_HEADS * A_V_DIM
B_W = B_HEADS * HEAD_DIM
EVEN_IN = 2 * A_QK + A_V + 3 * B_W

LANES = 128
EVEN_SLABS = EVEN_IN // LANES
MLA_PAIRS = MLA_HEADS // 2
V7X_VMEM_BYTES = 64 * 1024 * 1024

ROW_TILE = 1024
MLP_ROW_TILE = 1024
FF_CHUNK = 1024
Q_TILE = 128
MLA_STEP_PAIRS = 2
EVEN_STEP_SLABS = 2

F32 = jnp.float32
BF16 = jnp.bfloat16
LOG2E = math.log2(math.e)


def _nbytes(shape, dtype):
    return math.prod(1 if d is None else d for d in shape) * jnp.dtype(dtype).itemsize


def _call(body, *, name, grid, in_specs, out_specs, out_shape, args, temp_bytes, scratch_shapes=(), resident=()):
    outs = out_shape if isinstance(out_shape, (list, tuple)) else [out_shape]
    out_list = out_specs if isinstance(out_specs, (list, tuple)) else [out_specs]
    sizes = [_nbytes(spec.block_shape, a.dtype) for spec, a in zip([*in_specs, *out_list], [*args, *outs])]
    windows = sum(size * (1 if i in resident else 2) for i, size in enumerate(sizes))
    limit = windows + sum(_nbytes(m.shape, m.dtype) for m in scratch_shapes) + temp_bytes
    assert limit <= V7X_VMEM_BYTES, (name, limit)
    return pl.pallas_call(
        body, grid=grid, in_specs=in_specs, out_specs=out_specs, out_shape=out_shape, scratch_shapes=list(scratch_shapes),
        compiler_params=pltpu.CompilerParams(vmem_limit_bytes=limit), name=name,
    )(*args)


def _rms(x, g):
    return x * lax.rsqrt(jnp.mean(x * x, axis=-1, keepdims=True) + EPS) * g


def _rope(x, c, sa, sb, half):
    return x * c + pltpu.roll(x, half, 1) * sa + pltpu.roll(x, LANES - half, 1) * sb


def _rope_angles(seq, rot_dim):
    pos = jnp.arange(seq, dtype=F32)
    inv = ROPE_THETA ** (-jnp.arange(0, rot_dim, 2, dtype=F32) / rot_dim)
    ang = pos[:, None] * inv[None, :]
    return jnp.cos(ang), jnp.sin(ang)


def _rope_paired(x, c, s):
    return x * c + pltpu.roll(x, LANES - MLA_ROPE, 1) * s


def _mla_rope_tables(seq):
    cos, sin = _rope_angles(seq, MLA_ROPE)
    ones = jnp.ones((seq, MLA_NOPE), F32)
    zeros = jnp.zeros((seq, MLA_ROPE), F32)
    c = jnp.concatenate([ones, cos, cos, zeros], axis=-1)
    s = jnp.concatenate([jnp.zeros_like(ones), -sin, sin, zeros], axis=-1)
    return c, s


def _rope_tables(seq):
    cos, sin = _rope_angles(seq, ROT_DIM)
    zh = jnp.zeros_like(sin)
    rest = HEAD_DIM - ROT_DIM
    c = jnp.concatenate([cos, cos, jnp.ones((seq, rest), F32)], axis=-1)
    sa = jnp.concatenate([zh, sin, jnp.zeros((seq, rest), F32)], axis=-1)
    sb = jnp.concatenate([-sin, zh, jnp.zeros((seq, rest), F32)], axis=-1)
    return tuple(jnp.tile(t, (1, LANES // HEAD_DIM)) for t in (c, sa, sb))


def _even_proj_kernel(x_ref, g_ref, w_ref, c_ref, sa_ref, sb_ref, o_ref):
    h = _rms(x_ref[0], g_ref[...]).astype(BF16)
    c, sa, sb = c_ref[...], sa_ref[...], sb_ref[...]
    group = 4
    for j0 in range(0, EVEN_SLABS, group):
        p = jnp.dot(h, w_ref[:, j0 * LANES:(j0 + group) * LANES], preferred_element_type=F32)
        for jj in range(group):
            j = j0 + jj
            blk = p[:, jj * LANES:(jj + 1) * LANES]
            if j < 2 * A_HEADS:
                blk = _rope(blk, c, sa, sb, ROT_DIM // 2)
            if j < A_HEADS or 3 * A_HEADS <= j < 4 * A_HEADS:
                blk = blk * (HEAD_DIM ** -0.5 * LOG2E)
            o_ref[0, j] = blk.astype(BF16)


def _even_proj(x, g, w, tables, layer, idx):
    B, S, _ = x.shape
    nt = S // ROW_TILE
    tab_spec = pl.BlockSpec((ROW_TILE, LANES), lambda b, t: (t, 0))
    temp_bytes = _nbytes((ROW_TILE, D_MODEL), BF16) + 4 * _nbytes((ROW_TILE, 4 * LANES), F32)
    return _call(
        _even_proj_kernel,
        grid=(B, nt),
        in_specs=[
            pl.BlockSpec((1, ROW_TILE, D_MODEL), lambda b, t: (b, t, 0)),
            pl.BlockSpec((None, 1, D_MODEL), lambda b, t: (layer, 0, 0)),
            pl.BlockSpec((None, D_MODEL, EVEN_IN), lambda b, t: (idx, 0, 0)),
            tab_spec, tab_spec, tab_spec,
        ],
        out_specs=pl.BlockSpec((1, EVEN_SLABS, ROW_TILE, LANES), lambda b, t: (b, 0, t, 0)),
        out_shape=jax.ShapeDtypeStruct((B, EVEN_SLABS, S, LANES), BF16),
        args=(x, g, w, *tables), temp_bytes=temp_bytes, resident=(1, 2), name="even_proj",
    )


def _softmax_rows(s):
    return jnp.exp2(s - jnp.max(s, axis=-1, keepdims=True)).astype(BF16)


def _scores(q, k):
    return lax.dot_general(q, k, (((1,), (1,)), ((), ())), preferred_element_type=F32)


def _score_temp_bytes(rows, keys):
    return 4 * (_nbytes((rows, keys), F32) + _nbytes((rows, keys), BF16))


def _low_lanes(rows):
    return lax.broadcasted_iota(jnp.int32, (rows, LANES), 1) < HEAD_DIM


def _stage_values(v, vx_ref):
    vx_ref[:, :LANES] = v
    vx_ref[:, LANES:] = jnp.ones_like(v)


def _weighted_values(p, vx):
    res = jnp.dot(p, vx, preferred_element_type=F32)
    return res[:, :LANES] / res[:, LANES:]


def _diff_attn_kernel(q_ref, k_ref, v_ref, lp_ref, g_ref, o_ref, vx_ref, *, lam_init, seq):
    for hd in range(EVEN_STEP_SLABS):
        _stage_values(v_ref[0, hd], vx_ref.at[hd])
    lp = lp_ref[...]
    lam = (jnp.exp(jnp.sum(lp[0:1] * lp[1:2], axis=-1, keepdims=True))
           - jnp.exp(jnp.sum(lp[2:3] * lp[3:4], axis=-1, keepdims=True)) + lam_init)
    lo = _low_lanes(Q_TILE)
    gain = g_ref[...] * (1.0 - lam_init)

    def body(i, carry):
        r = pl.multiple_of(i * Q_TILE, Q_TILE)
        for hd in range(EVEN_STEP_SLABS):
            softmax_pv = lambda q: _weighted_values(_softmax_rows(_scores(q, k_ref[0, hd])), vx_ref[hd])
            q = q_ref[0, hd, pl.ds(r, Q_TILE), :]
            zero = jnp.zeros_like(q)
            o = softmax_pv(jnp.where(lo, q, zero)) - lam * softmax_pv(jnp.where(lo, zero, q))
            o_ref[0, pl.ds(r, Q_TILE), hd * LANES:(hd + 1) * LANES] = _rms(o, gain).astype(BF16)
        return carry

    lax.fori_loop(0, seq // Q_TILE, body, 0, unroll=True)


def _diff_attn(proj, lam_params, subln_g, lam_init):
    B, _, S, _ = proj.shape
    n = EVEN_STEP_SLABS
    slab = lambda off: pl.BlockSpec((1, n, S, LANES), lambda b, h: (b, off // n + h, 0, 0))
    return _call(
        functools.partial(_diff_attn_kernel, lam_init=lam_init, seq=S),
        grid=(B, A_HEADS // n),
        in_specs=[
            slab(0), slab(A_HEADS), slab(2 * A_HEADS),
            pl.BlockSpec((4, HEAD_DIM), lambda b, h: (0, 0)),
            pl.BlockSpec((1, A_V_DIM), lambda b, h: (0, 0)),
        ],
        out_specs=pl.BlockSpec((1, S, n * LANES), lambda b, h: (b, 0, h)),
        out_shape=jax.ShapeDtypeStruct((B, S, A_V), BF16),
        scratch_shapes=[pltpu.VMEM((n, S, 2 * LANES), BF16)],
        args=(proj, proj, proj, lam_params, subln_g), temp_bytes=_score_temp_bytes(Q_TILE, S), name="diff_attn",
    )


NAT_ROWS = 4
NAT_WIN = 12
NEG_BIG = -1e30


def _nat_window_start(g, rows):
    return jnp.clip(NAT_ROWS * g - WIN_R // 2, 0, rows - NAT_WIN)


def _natten_bias(rpb, rows):
    heads = rpb.shape[0]
    groups = rows // NAT_ROWS
    c = np.arange(GRID_W)
    cs = np.clip(c - WIN_C // 2, 0, GRID_W - WIN_C)
    col_mask = (c[None, :] >= cs[:, None]) & (c[None, :] < cs[:, None] + WIN_C)
    col_idx = np.clip(c[None, :] - c[:, None], -(WIN_C - 1), WIN_C - 1) + (WIN_C - 1)
    onehot = (col_idx[None] == np.arange(2 * WIN_C - 1)[:, None, None]).astype(np.float32)
    colbias = jnp.einsum("hrj,jqk->hqrk", rpb.astype(F32), jnp.asarray(onehot), precision=lax.Precision.HIGHEST)
    colbias = jnp.where(jnp.asarray(col_mask)[None, :, None], colbias * LOG2E, NEG_BIG)
    slabs = []
    for g in (0, 1, groups - 1):
        win0 = int(np.clip(NAT_ROWS * g - WIN_R // 2, 0, rows - NAT_WIN))
        for a in range(NAT_ROWS):
            r = NAT_ROWS * g + a
            r0 = int(np.clip(r - WIN_R // 2, 0, rows - WIN_R))
            first = r0 - r + WIN_R - 1
            before = r0 - win0
            slabs.append(jnp.pad(colbias[:, :, first:first + WIN_R], ((0, 0), (0, 0), (before, NAT_WIN - WIN_R - before), (0, 0)),
                                 constant_values=NEG_BIG))
    return jnp.stack(slabs, axis=1).reshape(heads, 3, NAT_ROWS * GRID_W, NAT_WIN * GRID_W)


def _natten_kernel(q_ref, k_ref, v_ref, b_ref, o_ref, vx_ref, *, rows):
    for pr in range(EVEN_STEP_SLABS):
        _stage_values(v_ref[0, pr], vx_ref.at[pr])
    qt = NAT_ROWS * GRID_W
    win = NAT_WIN * GRID_W
    lo = _low_lanes(qt)

    def body(g, carry):
        win0 = _nat_window_start(g, rows)
        variant = lax.shift_right_logical(NAT_ROWS * g - win0, int(math.log2(NAT_ROWS)))
        qstart = pl.multiple_of(g * qt, qt)
        kstart = pl.multiple_of(win0 * GRID_W, GRID_W)
        for pr in range(EVEN_STEP_SLABS):
            q = q_ref[0, pr, pl.ds(qstart, qt), :]
            ks = k_ref[0, pr, pl.ds(kstart, win), :]
            vs = vx_ref[pr, pl.ds(kstart, win), :]
            zero = jnp.zeros_like(q)
            outs = []
            for hd in range(2):
                qm = jnp.where(lo, q, zero) if hd == 0 else jnp.where(lo, zero, q)
                outs.append(_weighted_values(_softmax_rows(_scores(qm, ks) + b_ref[2 * pr + hd, variant]), vs))
            o_ref[0, pl.ds(qstart, qt), pr * LANES:(pr + 1) * LANES] = jnp.where(lo, outs[0], outs[1]).astype(BF16)
        return carry

    lax.fori_loop(0, rows // NAT_ROWS, body, 0, unroll=True)


def _natten(proj, bias, idx):
    B, _, S, _ = proj.shape
    rows = S // GRID_W
    assert NAT_ROWS == WIN_R // 2 and rows % NAT_ROWS == 0 and rows >= NAT_WIN
    base = 3 * A_HEADS
    pairs = B_HEADS // 2
    n = EVEN_STEP_SLABS
    steps = pairs // n
    slab = lambda off: pl.BlockSpec((1, n, S, LANES), lambda p, b: (b, off // n + p, 0, 0))
    return _call(
        functools.partial(_natten_kernel, rows=rows),
        grid=(steps, B),
        in_specs=[
            slab(base), slab(base + pairs), slab(base + 2 * pairs),
            pl.BlockSpec((2 * n, 3, NAT_ROWS * GRID_W, NAT_WIN * GRID_W), lambda p, b: (idx * steps + p, 0, 0, 0)),
        ],
        out_specs=pl.BlockSpec((1, S, n * LANES), lambda p, b: (b, 0, p)),
        out_shape=jax.ShapeDtypeStruct((B, S, B_W), BF16),
        scratch_shapes=[pltpu.VMEM((n, S, 2 * LANES), BF16)],
        args=(proj, proj, proj, bias), temp_bytes=_score_temp_bytes(NAT_ROWS * GRID_W, NAT_WIN * GRID_W), name="natten",
    )


def _even_attn_kernel(qa_ref, ka_ref, va_ref, lp_ref, g_ref, qb_ref, kb_ref, vb_ref, b_ref, oa_ref, ob_ref, vxa_ref, vxb_ref,
                      *, lam_init, seq, rows):
    _diff_attn_kernel(qa_ref, ka_ref, va_ref, lp_ref, g_ref, oa_ref, vxa_ref, lam_init=lam_init, seq=seq)
    _natten_kernel(qb_ref, kb_ref, vb_ref, b_ref, ob_ref, vxb_ref, rows=rows)


def _even_attn(proj, lam_params, subln_g, lam_init, bias, idx):
    B, _, S, _ = proj.shape
    rows = S // GRID_W
    assert NAT_ROWS == WIN_R // 2 and rows % NAT_ROWS == 0 and rows >= NAT_WIN
    n = EVEN_STEP_SLABS
    pairs = B_HEADS // 2
    assert A_HEADS // n == pairs // n
    steps = pairs // n
    base = 3 * A_HEADS
    slab = lambda off: pl.BlockSpec((1, n, S, LANES), lambda p, b: (b, off // n + p, 0, 0))
    out = pl.BlockSpec((1, S, n * LANES), lambda p, b: (b, 0, p))
    scratch = pltpu.VMEM((n, S, 2 * LANES), BF16)
    temp_bytes = _score_temp_bytes(Q_TILE, S) + _score_temp_bytes(NAT_ROWS * GRID_W, NAT_WIN * GRID_W)
    return _call(
        functools.partial(_even_attn_kernel, lam_init=lam_init, seq=S, rows=rows),
        grid=(steps, B),
        in_specs=[
            slab(0), slab(A_HEADS), slab(2 * A_HEADS),
            pl.BlockSpec((4, HEAD_DIM), lambda p, b: (0, 0)),
            pl.BlockSpec((1, A_V_DIM), lambda p, b: (0, 0)),
            slab(base), slab(base + pairs), slab(base + 2 * pairs),
            pl.BlockSpec((2 * n, 3, NAT_ROWS * GRID_W, NAT_WIN * GRID_W), lambda p, b: (idx * steps + p, 0, 0, 0)),
        ],
        out_specs=[out, out],
        out_shape=[jax.ShapeDtypeStruct((B, S, A_V), BF16), jax.ShapeDtypeStruct((B, S, B_W), BF16)],
        scratch_shapes=[scratch, scratch],
        args=(proj, proj, proj, lam_params, subln_g, proj, proj, proj, bias), temp_bytes=temp_bytes, name="even_attn",
    )


def _mla_proj_kernel(x_ref, g_ref, win_ref, gq_ref, gkv_ref, wuq_ref, wuk_ref, wuv_ref,
                     c_ref, s_ref, q_ref, k_ref, v_ref):
    h = _rms(x_ref[0], g_ref[...]).astype(BF16)
    c, s = c_ref[...], s_ref[...]
    lat = jnp.dot(h, win_ref[...], preferred_element_type=F32)
    cq = _rms(lat[:, :Q_LORA], gq_ref[...]).astype(BF16)
    ckv = _rms(lat[:, Q_LORA:Q_LORA + KV_LORA], gkv_ref[...]).astype(BF16)
    k_rope = _rope_paired(lat[:, Q_LORA + KV_LORA:], c, s)
    scale = (MLA_NOPE + MLA_ROPE) ** -0.5 * LOG2E
    group = 4
    for j0 in range(0, MLA_HEADS, group):
        cols = slice(j0 * LANES, (j0 + group) * LANES)
        qg = jnp.dot(cq, wuq_ref[:, cols], preferred_element_type=F32)
        kg = jnp.dot(ckv, wuk_ref[:, cols], preferred_element_type=F32)
        for jj in range(group):
            sl = slice(jj * LANES, (jj + 1) * LANES)
            q_ref[0, j0 + jj] = (_rope_paired(qg[:, sl], c, s) * scale).astype(BF16)
            k_ref[0, j0 + jj] = (kg[:, sl] + k_rope).astype(BF16)
    vv = jnp.dot(ckv, wuv_ref[...], preferred_element_type=F32)
    for j in range(MLA_PAIRS):
        v_ref[0, j] = vv[:, j * LANES:(j + 1) * LANES].astype(BF16)


def _mla_proj(x, g, w_in, gq, gkv, w_uq, w_uk, w_uv, tables, layer, idx):
    B, S, _ = x.shape
    nt = S // ROW_TILE
    tab_spec = pl.BlockSpec((ROW_TILE, LANES), lambda b, t: (t, 0))
    whole = lambda a: pl.BlockSpec((None,) + a.shape[1:], lambda b, t: (idx,) + (0,) * (a.ndim - 1))
    head_out = lambda n: pl.BlockSpec((1, n, ROW_TILE, LANES), lambda b, t: (b, 0, t, 0))
    temp_bytes = (_nbytes((ROW_TILE, D_MODEL), BF16) + _nbytes((ROW_TILE, w_in.shape[-1]), F32)
                  + 4 * _nbytes((ROW_TILE, 4 * LANES), F32) + _nbytes((ROW_TILE, w_uv.shape[-1]), F32))
    return _call(
        _mla_proj_kernel,
        grid=(B, nt),
        in_specs=[
            pl.BlockSpec((1, ROW_TILE, D_MODEL), lambda b, t: (b, t, 0)),
            pl.BlockSpec((None, 1, D_MODEL), lambda b, t: (layer, 0, 0)),
            whole(w_in), whole(gq), whole(gkv), whole(w_uq), whole(w_uk), whole(w_uv),
            tab_spec, tab_spec,
        ],
        out_specs=[head_out(MLA_HEADS), head_out(MLA_HEADS), head_out(MLA_PAIRS)],
        out_shape=[
            jax.ShapeDtypeStruct((B, MLA_HEADS, S, LANES), BF16),
            jax.ShapeDtypeStruct((B, MLA_HEADS, S, LANES), BF16),
            jax.ShapeDtypeStruct((B, MLA_PAIRS, S, LANES), BF16),
        ],
        args=(x, g, w_in, gq, gkv, w_uq, w_uk, w_uv, *tables), temp_bytes=temp_bytes, resident=range(1, 8), name="mla_proj",
    )


def _mla_attn_kernel(q_ref, k_ref, v_ref, o_ref, vx_ref, *, seq):
    for pr in range(MLA_STEP_PAIRS):
        _stage_values(v_ref[0, pr], vx_ref.at[pr])
    lo = _low_lanes(Q_TILE)

    def body(i, carry):
        r = pl.multiple_of(i * Q_TILE, Q_TILE)
        for pr in range(MLA_STEP_PAIRS):
            outs = [_weighted_values(_softmax_rows(_scores(q_ref[0, 2 * pr + hd, pl.ds(r, Q_TILE), :], k_ref[0, 2 * pr + hd])),
                                     vx_ref[pr]) for hd in range(2)]
            o_ref[0, pl.ds(r, Q_TILE), pr * LANES:(pr + 1) * LANES] = jnp.where(lo, outs[0], outs[1]).astype(BF16)
        return carry

    lax.fori_loop(0, seq // Q_TILE, body, 0, unroll=True)


def _mla_attn(q, k, v):
    B, _, S, _ = q.shape
    n = MLA_STEP_PAIRS
    heads = pl.BlockSpec((1, 2 * n, S, LANES), lambda b, p: (b, p, 0, 0))
    return _call(
        functools.partial(_mla_attn_kernel, seq=S),
        grid=(B, MLA_PAIRS // n),
        in_specs=[heads, heads, pl.BlockSpec((1, n, S, LANES), lambda b, p: (b, p, 0, 0))],
        out_specs=pl.BlockSpec((1, S, n * LANES), lambda b, p: (b, 0, p)),
        out_shape=jax.ShapeDtypeStruct((B, S, MLA_HEADS * MLA_V), BF16),
        scratch_shapes=[pltpu.VMEM((n, S, 2 * LANES), BF16)],
        args=(q, k, v), temp_bytes=_score_temp_bytes(Q_TILE, S), name="mla_attn",
    )


def _out_mlp_kernel(*refs, n_parts, final):
    x_ref = refs[0]
    part_refs = refs[1:1 + n_parts]
    wo_ref, g_ref, wu_ref, wd_ref = refs[1 + n_parts:5 + n_parts]
    gf_ref = refs[5 + n_parts] if final else None
    o_ref = refs[-1]
    attn = jnp.concatenate([p_ref[0] for p_ref in part_refs], axis=-1)
    x = x_ref[0] + jnp.dot(attn, wo_ref[...], preferred_element_type=F32)
    h = _rms(x, g_ref[...]).astype(BF16)
    acc = x
    for c0 in range(0, D_FF, FF_CHUNK):
        u = jnp.dot(h, wu_ref[:, c0:c0 + FF_CHUNK], preferred_element_type=F32)
        u = jnp.square(jnp.maximum(u, 0.0)).astype(BF16)
        acc = acc + jnp.dot(u, wd_ref[c0:c0 + FF_CHUNK, :], preferred_element_type=F32)
    if final:
        acc = _rms(acc, gf_ref[...])
    o_ref[0] = acc


def _out_mlp(x, parts, w_out, idx, g, w_up, w_down, layer, final_g):
    B, S, _ = x.shape
    nt = S // MLP_ROW_TILE
    final = final_g is not None
    row = lambda width: pl.BlockSpec((1, MLP_ROW_TILE, width), lambda b, t: (b, t, 0))
    in_specs = [row(D_MODEL)] + [row(p.shape[-1]) for p in parts] + [
        pl.BlockSpec((None, D_MODEL, D_MODEL), lambda b, t: (idx, 0, 0)),
        pl.BlockSpec((None, 1, D_MODEL), lambda b, t: (layer, 0, 0)),
        pl.BlockSpec((None, D_MODEL, D_FF), lambda b, t: (layer, 0, 0)),
        pl.BlockSpec((None, D_FF, D_MODEL), lambda b, t: (layer, 0, 0)),
    ]
    args = [x, *parts, w_out, g, w_up, w_down]
    if final:
        in_specs.append(pl.BlockSpec((1, D_MODEL), lambda b, t: (0, 0)))
        args.append(final_g)
    temp_bytes = (2 * _nbytes((MLP_ROW_TILE, D_MODEL), F32) + _nbytes((MLP_ROW_TILE, D_MODEL), BF16)
                  + _nbytes((MLP_ROW_TILE, FF_CHUNK), F32) + _nbytes((MLP_ROW_TILE, FF_CHUNK), BF16))
    weights = range(1 + len(parts), len(args))
    return _call(
        functools.partial(_out_mlp_kernel, n_parts=len(parts), final=final),
        grid=(B, nt),
        in_specs=in_specs,
        out_specs=row(D_MODEL),
        out_shape=jax.ShapeDtypeStruct((B, S, D_MODEL), F32),
        args=args, temp_bytes=temp_bytes, resident=weights, name="out_mlp",
    )


def _with_swapped_rope(w):
    half = MLA_ROPE // 2
    return jnp.concatenate([w, w[..., -half:], w[..., -MLA_ROPE:-half]], axis=-1)


def _mla_weight_layouts(od_w_in, od_w_uq, od_w_ukv):
    assert MLA_NOPE + 2 * MLA_ROPE == LANES
    n = od_w_in.shape[0]
    pad_lo = jnp.zeros((n, D_MODEL, MLA_NOPE), F32)
    w_in = jnp.concatenate([od_w_in[:, :, :Q_LORA + KV_LORA], pad_lo, _with_swapped_rope(od_w_in[:, :, Q_LORA + KV_LORA:])], axis=-1)
    w_uq = _with_swapped_rope(od_w_uq.reshape(n, Q_LORA, MLA_HEADS, MLA_NOPE + MLA_ROPE)).reshape(n, Q_LORA, MLA_HEADS * LANES)
    w_ukv = od_w_ukv.reshape(n, KV_LORA, MLA_HEADS, MLA_NOPE + MLA_V)
    w_uk = jnp.pad(w_ukv[..., :MLA_NOPE], ((0, 0), (0, 0), (0, 0), (0, LANES - MLA_NOPE))).reshape(n, KV_LORA, MLA_HEADS * LANES)
    w_uv = w_ukv[..., MLA_NOPE:].reshape(n, KV_LORA, MLA_HEADS * MLA_V)
    return w_in.astype(BF16), w_uq.astype(BF16), w_uk.astype(BF16), w_uv.astype(BF16)


def kernel(x, ln_mix_g, ln_mlp_g, w_up, w_down, ln_f_g, ev_w_in, ev_w_out, ev_lambda_q1, ev_lambda_k1, ev_lambda_q2, ev_lambda_k2, ev_subln_g, ev_rpb, od_w_in, od_q_norm_g, od_kv_norm_g, od_w_uq, od_w_ukv, od_w_out):
    B, S, D = x.shape
    assert D == D_MODEL and S % ROW_TILE == 0 and S % MLP_ROW_TILE == 0 and S % Q_TILE == 0 and S % GRID_W == 0
    x = x.astype(F32)
    ln_mix = ln_mix_g.astype(F32)[:, None, :]
    ln_mlp = ln_mlp_g.astype(F32)[:, None, :]
    w_up_b, w_down_b = w_up.astype(BF16), w_down.astype(BF16)
    ev_w_in_b, ev_w_out_b, od_w_out_b = ev_w_in.astype(BF16), ev_w_out.astype(BF16), od_w_out.astype(BF16)
    od_w_in_b, od_w_uq_b, od_w_uk_b, od_w_uv_b = _mla_weight_layouts(od_w_in.astype(F32), od_w_uq.astype(F32), od_w_ukv.astype(F32))
    gq = od_q_norm_g.astype(F32)[:, None, :]
    gkv = od_kv_norm_g.astype(F32)[:, None, :]
    even_tables = _rope_tables(S)
    mla_tables = _mla_rope_tables(S)
    final_g = ln_f_g.astype(F32)[None, :]
    nat_bias = _natten_bias(ev_rpb.reshape(-1, *ev_rpb.shape[2:]), S // GRID_W)

    for l in range(DEPTH):
        i = l // 2
        last = final_g if l == DEPTH - 1 else None
        if l % 2 == 0:
            lam_init = 0.8 - 0.6 * math.exp(-0.3 * l)
            proj = _even_proj(x, ln_mix, ev_w_in_b, even_tables, l, i)
            lam_params = jnp.stack([ev_lambda_q1[i], ev_lambda_k1[i], ev_lambda_q2[i], ev_lambda_k2[i]]).astype(F32)
            o_a, o_b = _even_attn(proj, lam_params, ev_subln_g[i].astype(F32)[None, :], lam_init, nat_bias, i)
            parts, w_out, idx = [o_a, o_b], ev_w_out_b, i
        else:
            q, k, v = _mla_proj(x, ln_mix, od_w_in_b, gq, gkv, od_w_uq_b, od_w_uk_b, od_w_uv_b, mla_tables, l, i)
            parts, w_out, idx = [_mla_attn(q, k, v)], od_w_out_b, i
        x = _out_mlp(x, parts, w_out, idx, ln_mlp, w_up_b, w_down_b, l, last)
    return x
```
